```python
import jax, jax.numpy as jnp
from jax import lax
import numpy as np

D_MODEL = 2048
BATCH = 2
SEQ = 4096
DEPTH = 1
DEC_BATCH = 32
DEC_SEQ = 4
PAST_LEN = 16384
PAGE_SIZE = 128

N_META = 16
N_HEADS = 8
HEAD_DIM = 128
D_ATTN = N_HEADS * HEAD_DIM
D_CONV = D_MODEL - D_ATTN
D_MIX = D_ATTN + D_CONV
N_IDX_HEADS = 16
D_IDX = 128
TOPK_MAX = 256
CONV_W = 31
Q_BLOCK = 128
LN_EPS = 1e-5
ATTN_SCALE = HEAD_DIM ** -0.5
IDX_SCALE = (N_IDX_HEADS * D_IDX) ** -0.5
ALPHA = (2.0 * DEPTH) ** 0.25
BETA = (8.0 * DEPTH) ** -0.25
SPLITS = (D_ATTN, D_ATTN, D_ATTN, N_IDX_HEADS * D_IDX, D_IDX, N_IDX_HEADS, D_ATTN, 2 * D_CONV, D_CONV)
D_IN = sum(SPLITS)

kernel_name = "hymba_dsa_conformer_deepnorm_step"


def layer_norm(x, g, b):
    xf = x.astype(jnp.float32)
    mu = jnp.mean(xf, axis=-1, keepdims=True)
    var = jnp.mean(jnp.square(xf - mu), axis=-1, keepdims=True)
    return ((xf - mu) * lax.rsqrt(var + LN_EPS) * g + b).astype(x.dtype)


def gather_rows(rows, idx):
    return jax.vmap(lambda r, i: r[i])(rows, idx)


def in_projection(h, w_in, kidx_g, kidx_b):
    B, T, _ = h.shape
    z = jnp.einsum("btd,de->bte", h, w_in)
    offsets = [int(o) for o in np.cumsum(SPLITS)[:-1]]
    q, k, v, qi, ki, wi, g_attn, glu, g_conv = jnp.split(z, offsets, axis=-1)
    q = q.reshape(B, T, N_HEADS, HEAD_DIM)
    k = k.reshape(B, T, N_HEADS, HEAD_DIM)
    v = v.reshape(B, T, N_HEADS, HEAD_DIM)
    qi = qi.reshape(B, T, N_IDX_HEADS, D_IDX)
    ki = layer_norm(ki, kidx_g, kidx_b)
    wi = wi * IDX_SCALE
    u = glu[..., :D_CONV] * jax.nn.sigmoid(glu[..., D_CONV:])
    return q, k, v, qi, ki, wi, g_attn, u, g_conv


def conv_branch(u_ext, conv_w, conv_b, g, b, gate):
    y = lax.conv_general_dilated(u_ext, conv_w[:, None, :], window_strides=(1,), padding="VALID",
                                 dimension_numbers=("NWC", "WIO", "NWC"),
                                 feature_group_count=D_CONV) + conv_b
    y = layer_norm(y, g, b)
    return jax.nn.silu(y) * jax.nn.silu(gate)


def indexer_scores(qi, ki, wi):
    s = jax.nn.relu(jnp.einsum("bqhd,bsd->bqhs", qi, ki))
    return jnp.einsum("bqhs,bqh->bqs", s, wi).astype(jnp.float32)


def attend_selected(q, k_sel, v_sel, valid):
    s = jnp.einsum("bqhd,bqkhd->bqhk", q, k_sel).astype(jnp.float32) * ATTN_SCALE
    s = jnp.where(valid[:, :, None, :], s, -jnp.inf)
    p = jax.nn.softmax(s, axis=-1).astype(v_sel.dtype)
    return jnp.einsum("bqhk,bqkhd->bqhd", p, v_sel)


def prompt_sparse_attention(q, k, v, qi, ki, wi, n_sel):
    B, T = q.shape[:2]
    n_blk = -(-T // Q_BLOCK)
    pad = n_blk * Q_BLOCK - T

    def blocks(a):
        a = jnp.pad(a, [(0, 0), (0, pad)] + [(0, 0)] * (a.ndim - 2))
        a = a.reshape((B, n_blk, Q_BLOCK) + a.shape[2:])
        return jnp.moveaxis(a, 1, 0)

    key_pos = jnp.arange(T)

    def one_block(args):
        blk, q_b, qi_b, wi_b = args
        t = blk * Q_BLOCK + jnp.arange(Q_BLOCK)
        score = indexer_scores(qi_b, ki, wi_b)
        score = jnp.where((key_pos[None, :] <= t[:, None])[None], score, -jnp.inf)
        _, idx = lax.top_k(score, n_sel)
        valid = idx <= t[None, :, None]
        return attend_selected(q_b, gather_rows(k, idx), gather_rows(v, idx), valid)

    out = lax.map(one_block, (jnp.arange(n_blk), blocks(q), blocks(qi), blocks(wi)))
    out = jnp.moveaxis(out, 0, 1).reshape(B, n_blk * Q_BLOCK, D_ATTN)
    return out[:, :T]


def sample_sparse_attention(q, k, v, qi, ki, wi, cache_k, cache_v, cache_kidx, page_table, n_sel):
    DB, S = q.shape[:2]
    past = page_table.shape[1] * PAGE_SIZE
    ki_past = cache_kidx[page_table].reshape(DB, past, D_IDX)
    ki_all = jnp.concatenate([ki_past.astype(ki.dtype), ki], axis=1)
    t = past + jnp.arange(S)
    score = indexer_scores(qi, ki_all, wi)
    score = jnp.where((jnp.arange(past + S)[None, :] <= t[:, None])[None], score, -jnp.inf)
    _, idx = lax.top_k(score, n_sel)
    valid = idx <= t[None, :, None]
    in_past = (idx < past)[..., None, None]
    p_idx = jnp.minimum(idx, past - 1)
    phys = gather_rows(page_table, p_idx // PAGE_SIZE)
    off = p_idx % PAGE_SIZE
    n_idx = jnp.clip(idx - past, 0, S - 1)
    k_sel = jnp.where(in_past, cache_k[phys, off].astype(k.dtype), gather_rows(k, n_idx))
    v_sel = jnp.where(in_past, cache_v[phys, off].astype(v.dtype), gather_rows(v, n_idx))
    return attend_selected(q, k_sel, v_sel, valid).reshape(DB, S, D_ATTN)


def setup_inputs(seed: int = 0) -> dict:
    key = jax.random.key(seed)
    ks = jax.random.split(key, 24)
    n_pages = PAST_LEN // PAGE_SIZE
    n_used = DEC_BATCH * n_pages
    n_pool = n_used + n_used // 4
    nrm = jax.random.normal
    f32 = jnp.float32
    page_table = jax.random.permutation(ks[6], n_pool)[:n_used].reshape(DEC_BATCH, n_pages).astype(jnp.int32)
    return {
        "x_prompt": nrm(ks[0], (BATCH, SEQ, D_MODEL), f32),
        "x_sample": nrm(ks[1], (DEC_BATCH, DEC_SEQ, D_MODEL), f32),
        "cache_k": nrm(ks[2], (DEPTH, n_pool, PAGE_SIZE, N_HEADS, HEAD_DIM), f32),
        "cache_v": nrm(ks[3], (DEPTH, n_pool, PAGE_SIZE, N_HEADS, HEAD_DIM), f32),
        "cache_kidx": nrm(ks[4], (DEPTH, n_pool, PAGE_SIZE, D_IDX), f32),
        "state_conv": 0.5 * nrm(ks[5], (DEPTH, DEC_BATCH, CONV_W - 1, D_CONV), f32),
        "page_table": page_table,
        "meta_tokens": nrm(ks[7], (N_META, D_MODEL), f32),
        "ln_in_g": 1.0 + 0.02 * nrm(ks[8], (D_MODEL,), f32),
        "ln_in_b": 0.02 * nrm(ks[9], (D_MODEL,), f32),
        "w_in": nrm(ks[10], (DEPTH, D_MODEL, D_IN), f32) * D_MODEL ** -0.5,
        "ln_kidx_g": 1.0 + 0.02 * nrm(ks[11], (DEPTH, D_IDX), f32),
        "ln_kidx_b": 0.02 * nrm(ks[12], (DEPTH, D_IDX), f32),
        "conv_w": nrm(ks[13], (DEPTH, CONV_W, D_CONV), f32) * CONV_W ** -0.5,
        "conv_b": 0.02 * nrm(ks[14], (DEPTH, D_CONV), f32),
        "ln_conv_g": 1.0 + 0.02 * nrm(ks[15], (DEPTH, D_CONV), f32),
        "ln_conv_b": 0.02 * nrm(ks[16], (DEPTH, D_CONV), f32),
        "w_out": nrm(ks[17], (DEPTH, D_MIX, D_MODEL), f32) * (D_MIX ** -0.5) * BETA,
        "ln_out_g": 1.0 + 0.02 * nrm(ks[18], (DEPTH, D_MODEL), f32),
        "ln_out_b": 0.02 * nrm(ks[19], (DEPTH, D_MODEL), f32),
    }


def reference(x_prompt, x_sample, cache_k, cache_v, cache_kidx, state_conv, page_table,
              meta_tokens, ln_in_g, ln_in_b, w_in, ln_kidx_g, ln_kidx_b, conv_w, conv_b,
              ln_conv_g, ln_conv_b, w_out, ln_out_g, ln_out_b):
    B = x_prompt.shape[0]
    DB, S = x_sample.shape[:2]
    n_sel_prompt = min(TOPK_MAX, x_prompt.shape[1] // 4)
    n_sel_sample = min(TOPK_MAX, (page_table.shape[1] * PAGE_SIZE + S) // 4)

    meta = jnp.broadcast_to(meta_tokens[None].astype(x_prompt.dtype), (B, N_META, D_MODEL))
    hp = layer_norm(jnp.concatenate([meta, x_prompt], axis=1), ln_in_g, ln_in_b)
    hs = layer_norm(x_sample, ln_in_g, ln_in_b)
    T = hp.shape[1]

    kp, vp, kip, cp, ks_, vs_, kis, cs = [], [], [], [], [], [], [], []
    for l in range(DEPTH):
        q, k, v, qi, ki, wi, g_attn, u, g_conv = in_projection(hp, w_in[l], ln_kidx_g[l], ln_kidx_b[l])
        a = prompt_sparse_attention(q, k, v, qi, ki, wi, n_sel_prompt) * jax.nn.silu(g_attn)
        u_ext = jnp.pad(u, ((0, 0), (CONV_W - 1, 0), (0, 0)))
        c = conv_branch(u_ext, conv_w[l], conv_b[l], ln_conv_g[l], ln_conv_b[l], g_conv)
        mix = jnp.concatenate([a, c], axis=-1)
        hp = layer_norm(ALPHA * hp + jnp.einsum("btm,md->btd", mix, w_out[l]), ln_out_g[l], ln_out_b[l])
        kp.append(k)
        vp.append(v)
        kip.append(ki)
        cp.append(u_ext[:, -(CONV_W - 1):])

        q, k, v, qi, ki, wi, g_attn, u, g_conv = in_projection(hs, w_in[l], ln_kidx_g[l], ln_kidx_b[l])
        a = sample_sparse_attention(q, k, v, qi, ki, wi, cache_k[l], cache_v[l], cache_kidx[l],
                                    page_table, n_sel_sample) * jax.nn.silu(g_attn)
        u_ext = jnp.concatenate([state_conv[l].astype(u.dtype), u], axis=1)
        c = conv_branch(u_ext, conv_w[l], conv_b[l], ln_conv_g[l], ln_conv_b[l], g_conv)
        mix = jnp.concatenate([a, c], axis=-1)
        hs = layer_norm(ALPHA * hs + jnp.einsum("btm,md->btd", mix, w_out[l]), ln_out_g[l], ln_out_b[l])
        ks_.append(k)
        vs_.append(v)
        kis.append(ki)
        cs.append(u_ext[:, -(CONV_W - 1):])

    y_prompt = hp[:, N_META:]
    return (y_prompt, hs, jnp.stack(kp), jnp.stack(vp), jnp.stack(kip), jnp.stack(cp),
            jnp.stack(ks_), jnp.stack(vs_), jnp.stack(kis), jnp.stack(cs))
```

```python
import functools

import jax
import jax.numpy as jnp
from jax import lax
from jax.experimental import pallas as pl
from jax.experimental.pallas import tpu as pltpu

N_META = 16
N_HEADS = 8
HEAD_DIM = 128
D_ATTN = N_HEADS * HEAD_DIM
N_IDX_HEADS = 16
D_IDX = 128
TOPK_MAX = 256
CONV_W = 31
PAGE_SIZE = 128
LN_EPS = 1e-5
ATTN_SCALE = HEAD_DIM ** -0.5
IDX_SCALE = (N_IDX_HEADS * D_IDX) ** -0.5

LANES = 128
SUBLANES = 8
NEG_BIG = -1e30
F32_MAX = 3.4028235e38
INT_MAX = 2 ** 31 - 1
SEARCH_CAP = 400
VMEM_LIMIT = 56 * 1024 * 1024

F32 = jnp.float32
BF16 = jnp.bfloat16
I32 = jnp.int32


def _params(*sem):
    return pltpu.CompilerParams(dimension_semantics=sem, vmem_limit_bytes=VMEM_LIMIT)


def _layer_norm(x, g, b):
    mu = jnp.mean(x, axis=-1, keepdims=True)
    xc = x - mu
    var = jnp.mean(xc * xc, axis=-1, keepdims=True)
    return xc * lax.rsqrt(var + LN_EPS) * g + b


def _sigmoid(z):
    return 1.0 / (1.0 + jnp.exp(-z))


def _dot_t(a, b):
    return lax.dot_general(a, b, (((1,), (1,)), ((), ())), preferred_element_type=F32)


def _fold_rows(x, reduce):
    return reduce(x.reshape(x.shape[0] // SUBLANES, SUBLANES, x.shape[1]), axis=0)


def _col_reduce(x, reduce):
    return reduce(_fold_rows(x, reduce), axis=0, keepdims=True)


def _topk_threshold(count, lo0, hi0, c0, active, k, n_total):
    one = jnp.ones(lo0.shape, I32)

    def cond(st):
        it, _, _, c_lo, stalled = st
        waiting = jnp.sum(jnp.where(active & (c_lo != k) & (stalled == 0), one, 0))
        return (it < SEARCH_CAP) & (waiting > 0)

    def body(st):
        it, lo, hi, c_lo, stalled = st
        mid = 0.5 * lo + 0.5 * hi
        stall = (mid <= lo) | (mid >= hi)
        c = count(lambda s, i: s >= mid)
        up = (c >= k) & jnp.logical_not(stall)
        down = (c < k) & jnp.logical_not(stall)
        return (it + 1, jnp.where(up, mid, lo), jnp.where(down, mid, hi), jnp.where(up, c, c_lo),
                jnp.where(stall, one, stalled))

    _, thr, _, c_lo, _ = lax.while_loop(cond, body, (jnp.int32(0), lo0, hi0, c0, jnp.zeros(lo0.shape, I32)))
    tied = active & (c_lo > k)

    def break_ties(_):
        need = k - count(lambda s, i: s > thr)

        def step(_, st):
            j_lo, j_hi = st
            j_mid = j_lo + (j_hi - j_lo) // 2
            ok = count(lambda s, i: (s == thr) & (i <= j_mid)) >= need
            return jnp.where(ok, j_lo, j_mid), jnp.where(ok, j_mid, j_hi)

        n_steps = (n_total + 1).bit_length()
        _, j_hi = lax.fori_loop(0, n_steps, step, (jnp.full(lo0.shape, -1, I32), jnp.full(lo0.shape, n_total, I32)))
        return jnp.where(tied, j_hi, INT_MAX)

    j_max = lax.cond(jnp.sum(jnp.where(tied, one, 0)) > 0, break_ties,
                     lambda _: jnp.full(lo0.shape, INT_MAX, I32), 0)
    thr = jnp.where(active, thr, -F32_MAX)
    return thr, j_max


def _above(x):
    return x + (jnp.abs(x) * 2.0 ** -20 + 1e-30)


def _ln_kernel(x_ref, g_ref, b_ref, o_ref):
    o_ref[...] = _layer_norm(x_ref[...], g_ref[...], b_ref[...]).astype(o_ref.dtype)


def _ln_cast(x, g, b, tm):
    m, d = x.shape
    return pl.pallas_call(
        _ln_kernel,
        out_shape=jax.ShapeDtypeStruct((m, d), BF16),
        grid=(m // tm,),
        in_specs=[pl.BlockSpec((tm, d), lambda i: (i, 0)),
                  pl.BlockSpec((1, d), lambda i: (0, 0)),
                  pl.BlockSpec((1, d), lambda i: (0, 0))],
        out_specs=pl.BlockSpec((tm, d), lambda i: (i, 0)),
        compiler_params=_params("parallel"),
        name="ln_in",
    )(x, g.reshape(1, d), b.reshape(1, d))


def _proj_q_kernel(h_ref, w_ref, o_ref):
    z = jnp.dot(h_ref[...], w_ref[...], preferred_element_type=F32)
    o_ref[...] = (z * ATTN_SCALE).astype(o_ref.dtype)


def _proj_cast_kernel(h_ref, w_ref, o_ref):
    o_ref[...] = jnp.dot(h_ref[...], w_ref[...], preferred_element_type=F32).astype(o_ref.dtype)


def _proj_kv_kernel(h_ref, w_ref, o32_ref, o16_ref):
    z = jnp.dot(h_ref[...], w_ref[...], preferred_element_type=F32)
    o32_ref[...] = z
    o16_ref[...] = z.astype(o16_ref.dtype)


def _proj_silu_kernel(h_ref, w_ref, o_ref):
    z = jnp.dot(h_ref[...], w_ref[...], preferred_element_type=F32)
    o_ref[...] = (z * _sigmoid(z)).astype(o_ref.dtype)


def _proj_glu_kernel(h_ref, wa_ref, wb_ref, o_ref):
    h = h_ref[...]
    za = jnp.dot(h, wa_ref[...], preferred_element_type=F32)
    zb = jnp.dot(h, wb_ref[...], preferred_element_type=F32)
    o_ref[...] = za * _sigmoid(zb)


def _proj_kiwi_kernel(h_ref, w_ref, g_ref, b_ref, ki32_ref, ki16_ref, wi_ref):
    z = jnp.dot(h_ref[...], w_ref[...], preferred_element_type=F32)
    ki = _layer_norm(z[:, :D_IDX], g_ref[...], b_ref[...])
    ki32_ref[...] = ki
    ki16_ref[...] = ki.astype(ki16_ref.dtype)
    wi_ref[...] = z[:, D_IDX:] * IDX_SCALE


def _proj(kernel, h, ws, extra, outs, tm, name):
    m, d = h.shape
    in_specs = [pl.BlockSpec((tm, d), lambda i: (i, 0))]
    in_specs += [pl.BlockSpec(w.shape, lambda i: (0, 0)) for w in ws]
    in_specs += [pl.BlockSpec(e.shape, lambda i: (0, 0)) for e in extra]
    out_shape = [jax.ShapeDtypeStruct((m, n), dt) for n, dt in outs]
    out_specs = [pl.BlockSpec((tm, n), lambda i: (i, 0)) for n, _ in outs]
    return pl.pallas_call(
        kernel,
        out_shape=out_shape,
        grid=(m // tm,),
        in_specs=in_specs,
        out_specs=out_specs,
        compiler_params=_params("parallel"),
        name=name,
    )(h, *ws, *extra)


def _in_projection(h, w, kidx_g, kidx_b, tm):
    (q,) = _proj(_proj_q_kernel, h, [w["q"]], [], [(D_ATTN, BF16)], tm, "proj_q")
    kv32, kv16 = _proj(_proj_kv_kernel, h, [w["kv"]], [], [(2 * D_ATTN, F32), (2 * D_ATTN, BF16)], tm, "proj_kv")
    (qi,) = _proj(_proj_cast_kernel, h, [w["qi"]], [], [(N_IDX_HEADS * D_IDX, BF16)], tm, "proj_qi")
    ki32, ki16, wi = _proj(_proj_kiwi_kernel, h, [w["kiwi"]], [kidx_g.reshape(1, D_IDX), kidx_b.reshape(1, D_IDX)],
                           [(D_IDX, F32), (D_IDX, BF16), (LANES, F32)], tm, "proj_kiwi")
    (ga,) = _proj(_proj_silu_kernel, h, [w["ga"]], [], [(D_ATTN, BF16)], tm, "proj_gattn")
    (u,) = _proj(_proj_glu_kernel, h, [w["glu_a"], w["glu_b"]], [], [(w["glu_a"].shape[1], F32)], tm, "proj_glu")
    (gc,) = _proj(_proj_silu_kernel, h, [w["gc"]], [], [(w["gc"].shape[1], BF16)], tm, "proj_gconv")
    return dict(q=q, kv32=kv32, kv16=kv16, qi=qi, ki32=ki32, ki16=ki16, wi=wi, ga=ga, u=u, gc=gc)


TQ = 256
KC = 256


def _prompt_attn_kernel(qi_ref, wi_ref, q_ref, ga_ref, kip_ref, kp_ref, vtp_ref, kim_ref, km_ref, vtm_ref,
                        o_ref, sc_ref, scm_ref, wt_ref, s_ref, m_ref, l_ref, acc_ref, *, n_sel):
    qb = pl.program_id(1)
    n_chunks = (qb * TQ + TQ - 1) // KC + 1
    q_pos = qb * TQ + lax.broadcasted_iota(I32, (1, TQ), 1)
    row_kc = lax.broadcasted_iota(I32, (KC, TQ), 0)
    row_m = lax.broadcasted_iota(I32, (LANES, TQ), 0)

    wt_ref[...] = jnp.transpose(wi_ref[...])

    def idx_scores(ki):
        acc = jnp.zeros((ki.shape[0], TQ), F32)
        for h in range(N_IDX_HEADS):
            s = _dot_t(ki, qi_ref[:, h * D_IDX:(h + 1) * D_IDX])
            acc = acc + jnp.maximum(s, 0.0) * wt_ref[h:h + 1, :]
        return acc

    sm = idx_scores(kim_ref[...])
    meta_vis = row_m < N_META
    scm_ref[...] = jnp.where(meta_vis, sm, -jnp.inf)

    def score_chunk(j, carry):
        rmax, rmin = carry
        r0 = pl.multiple_of(j * KC, KC)
        s = idx_scores(kip_ref[pl.ds(r0, KC), :])
        vis = (j * KC + row_kc) <= q_pos
        s_vis = jnp.where(vis, s, -jnp.inf)
        sc_ref[j] = s_vis
        rmax = jnp.maximum(rmax, _fold_rows(s_vis, jnp.max))
        rmin = jnp.minimum(rmin, _fold_rows(jnp.where(vis, s, jnp.inf), jnp.min))
        return rmax, rmin

    rmax, rmin = lax.fori_loop(0, n_chunks, score_chunk,
                               (_fold_rows(jnp.where(meta_vis, sm, -jnp.inf), jnp.max),
                                _fold_rows(jnp.where(meta_vis, sm, jnp.inf), jnp.min)))

    def count(f):
        def ones(mask):
            return _fold_rows(jnp.where(mask, 1, 0), jnp.sum)

        def body(j, c):
            return c + ones(f(sc_ref[j], LANES + j * KC + row_kc))

        c = lax.fori_loop(0, n_chunks, body, ones(f(scm_ref[...], row_m)))
        return jnp.sum(c, axis=0, keepdims=True)

    n_vis = N_META + 1 + q_pos
    thr, j_max = _topk_threshold(count, jnp.min(rmin, axis=0, keepdims=True),
                                 _above(jnp.max(rmax, axis=0, keepdims=True)), n_vis,
                                 n_vis > n_sel, n_sel, LANES + sc_ref.shape[0] * KC)

    def bias_of(s, idx):
        keep = (s > thr) | ((s == thr) & (idx <= j_max))
        return jnp.where(keep, 0.0, NEG_BIG)

    m_ref[...] = jnp.full(m_ref.shape, NEG_BIG, F32)
    l_ref[...] = jnp.zeros(l_ref.shape, F32)
    acc_ref[...] = jnp.zeros(acc_ref.shape, F32)

    def attend(bias, k_blk, vt_blk):
        n = bias.shape[0]
        heads = [slice(h * HEAD_DIM, (h + 1) * HEAD_DIM) for h in range(N_HEADS)]
        m_new = []
        for h, cs in enumerate(heads):
            s = _dot_t(k_blk[:, cs], q_ref[:, cs]) + bias
            s_ref[h, 0:n, :] = s
            m_new.append(jnp.maximum(m_ref[h], _col_reduce(s, jnp.max)))
        for h, cs in enumerate(heads):
            alpha = jnp.exp(m_ref[h] - m_new[h])
            p = jnp.exp(s_ref[h, 0:n, :] - m_new[h])
            l_ref[h] = alpha * l_ref[h] + _col_reduce(p, jnp.sum)
            pv = jnp.dot(vt_blk[cs, :], p.astype(BF16), preferred_element_type=F32)
            acc_ref[cs, :] = alpha * acc_ref[cs, :] + pv
            m_ref[h] = m_new[h]

    attend(bias_of(scm_ref[...], row_m), km_ref[...], vtm_ref[...])

    def attn_chunk(j, carry):
        r0 = pl.multiple_of(j * KC, KC)
        attend(bias_of(sc_ref[j], LANES + j * KC + row_kc), kp_ref[pl.ds(r0, KC), :], vtp_ref[j])
        return carry

    lax.fori_loop(0, n_chunks, attn_chunk, 0)

    for h in range(N_HEADS):
        cs = slice(h * HEAD_DIM, (h + 1) * HEAD_DIM)
        out = jnp.transpose(acc_ref[cs, :] / l_ref[h])
        o_ref[:, cs] = (out * ga_ref[:, cs].astype(F32)).astype(o_ref.dtype)


def _prompt_attention(p, meta, n_batch, seq, n_sel):
    def rows(a):
        return a.reshape(n_batch, seq, a.shape[-1])

    qi, wi, q, ga, ki16, kv16 = (rows(p[n]) for n in ("qi", "wi", "q", "ga", "ki16", "kv16"))
    n_kc = seq // KC
    vt = jnp.transpose(kv16[:, :, D_ATTN:].reshape(n_batch, n_kc, KC, D_ATTN), (0, 1, 3, 2))
    blk = lambda w: pl.BlockSpec((None, TQ, w), lambda b, i: (b, i, 0))
    res = lambda w: pl.BlockSpec((None, seq, w), lambda b, i: (b, 0, 0), pipeline_mode=pl.Buffered(1))
    const = lambda a: pl.BlockSpec(a.shape, lambda b, i: (0, 0), pipeline_mode=pl.Buffered(1))
    return pl.pallas_call(
        functools.partial(_prompt_attn_kernel, n_sel=n_sel),
        out_shape=jax.ShapeDtypeStruct((n_batch, seq, D_ATTN), BF16),
        grid=(n_batch, seq // TQ),
        in_specs=[blk(N_IDX_HEADS * D_IDX), blk(LANES), blk(D_ATTN), blk(D_ATTN),
                  res(D_IDX), res(D_ATTN),
                  pl.BlockSpec((None, n_kc, D_ATTN, KC), lambda b, i: (b, 0, 0, 0), pipeline_mode=pl.Buffered(1)),
                  const(meta["ki"]), const(meta["k"]), const(meta["vt"])],
        out_specs=blk(D_ATTN),
        scratch_shapes=[pltpu.VMEM((n_kc, KC, TQ), F32),
                        pltpu.VMEM((LANES, TQ), F32),
                        pltpu.VMEM((LANES, TQ), F32),
                        pltpu.VMEM((N_HEADS, KC, TQ), F32),
                        pltpu.VMEM((N_HEADS, 1, TQ), F32),
                        pltpu.VMEM((N_HEADS, 1, TQ), F32),
                        pltpu.VMEM((D_ATTN, TQ), F32)],
        compiler_params=_params("parallel", "arbitrary"),
        name="prompt_attn",
    )(qi, wi, q, ga, ki16, kv16, vt, meta["ki"], meta["k"], meta["vt"])


SCORE_PAGES = 32
ATTN_PAGES = 8


def _idx_rows(qi, wcol, ki, n_q):
    a = jnp.maximum(_dot_t(qi, ki), 0.0) * wcol
    return jnp.sum(a.reshape(n_q, N_IDX_HEADS, ki.shape[0]), axis=1)


def _sample_score_kernel(pt_ref, qi_ref, wcol_ref, kin_ref, *refs):
    pages, (o_ref, on_ref) = refs[:SCORE_PAGES], refs[SCORE_PAGES:]
    n_q = o_ref.shape[0]
    qi = qi_ref[...]
    wcol = wcol_ref[...]
    for r, page in enumerate(pages):
        o_ref[:, r * PAGE_SIZE:(r + 1) * PAGE_SIZE] = _idx_rows(qi, wcol, page[...].astype(BF16), n_q)

    @pl.when(pl.program_id(1) == pl.num_programs(1) - 1)
    def _():
        sc = _idx_rows(qi, wcol, kin_ref[...], n_q)
        lane = lax.broadcasted_iota(I32, (n_q, LANES), 1)
        row = lax.broadcasted_iota(I32, (n_q, LANES), 0)
        on_ref[...] = jnp.where(lane <= row, sc, -jnp.inf)


def _sample_scores(page_table, qi_rows, wcol, ki_new_pad, cache_kidx, layer):
    db, n_pages = page_table.shape
    n_rows = qi_rows.shape[1]
    n_q = n_rows // N_IDX_HEADS
    steps = n_pages // SCORE_PAGES
    per_b = lambda *shape: pl.BlockSpec((None,) + shape, lambda b, j, pt: (b, 0, 0))
    page_spec = lambda r: pl.BlockSpec((None, None, PAGE_SIZE, D_IDX),
                                       lambda b, j, pt, r=r: (layer, pt[b, j * SCORE_PAGES + r], 0, 0))
    grid_spec = pltpu.PrefetchScalarGridSpec(
        num_scalar_prefetch=1,
        grid=(db, steps),
        in_specs=[per_b(n_rows, D_IDX), per_b(n_rows, LANES), per_b(LANES, D_IDX)]
                 + [page_spec(r) for r in range(SCORE_PAGES)],
        out_specs=[pl.BlockSpec((None, n_q, SCORE_PAGES * PAGE_SIZE), lambda b, j, pt: (b, 0, j)),
                   per_b(n_q, LANES)],
    )
    return pl.pallas_call(
        _sample_score_kernel,
        out_shape=[jax.ShapeDtypeStruct((db, n_q, n_pages * PAGE_SIZE), F32),
                   jax.ShapeDtypeStruct((db, n_q, LANES), F32)],
        grid_spec=grid_spec,
        compiler_params=_params("parallel", "arbitrary"),
        name="sample_scores",
    )(page_table, qi_rows, wcol, ki_new_pad, *([cache_kidx] * SCORE_PAGES))


def _sample_thr_kernel(sp_ref, sn_ref, thr_ref, j_ref, *, n_sel, n_q):
    rows, past = sp_ref.shape
    lane = lax.broadcasted_iota(I32, (rows, LANES), 1)
    idx_p = lax.broadcasted_iota(I32, (rows, past), 1)
    sn = sn_ref[...]

    def count(f):
        c = jnp.sum(jnp.where(f(sp_ref[...], idx_p), 1, 0), axis=1, keepdims=True)
        return c + jnp.sum(jnp.where(f(sn, past + lane), 1, 0), axis=1, keepdims=True)

    new_vis = sn > -jnp.inf
    rmax = jnp.maximum(jnp.max(sp_ref[...], axis=1, keepdims=True), jnp.max(sn, axis=1, keepdims=True))
    rmin = jnp.minimum(jnp.min(sp_ref[...], axis=1, keepdims=True),
                       jnp.min(jnp.where(new_vis, sn, jnp.inf), axis=1, keepdims=True))
    n_vis = past + 1 + lax.rem(lax.broadcasted_iota(I32, (rows, 1), 0), n_q)
    thr, j_max = _topk_threshold(count, rmin, _above(rmax), n_vis, n_vis > n_sel, n_sel, past + LANES)
    thr_ref[...] = jnp.broadcast_to(thr, thr_ref.shape)
    j_ref[...] = jnp.broadcast_to(j_max, j_ref.shape)


def _sample_threshold(scores_past, scores_new, n_sel):
    db, n_q, past = scores_past.shape
    rows = db * n_q
    out = jax.ShapeDtypeStruct((rows, LANES), F32), jax.ShapeDtypeStruct((rows, LANES), I32)
    thr, j_max = pl.pallas_call(
        functools.partial(_sample_thr_kernel, n_sel=n_sel, n_q=n_q),
        out_shape=out,
        compiler_params=pltpu.CompilerParams(vmem_limit_bytes=VMEM_LIMIT),
        name="sample_threshold",
    )(scores_past.reshape(rows, past), scores_new.reshape(rows, LANES))
    return thr.reshape(db, n_q, LANES), j_max.reshape(db, n_q, LANES)


def _sample_attn_kernel(pt_ref, qrows_ref, sp_ref, sn_ref, thr_ref, jmax_ref, kn_ref, vn_ref, ga_ref, *refs,
                        n_q, past):
    k_pages = refs[:ATTN_PAGES]
    v_pages = refs[ATTN_PAGES:2 * ATTN_PAGES]
    o_ref, m_ref, l_ref, acc_ref = refs[2 * ATTN_PAGES:]
    j = pl.program_id(1)
    thr = thr_ref[:, 0:1]
    j_max = jmax_ref[:, 0:1]

    @pl.when(j == 0)
    def _():
        m_ref[...] = jnp.full(m_ref.shape, NEG_BIG, F32)
        l_ref[...] = jnp.zeros(l_ref.shape, F32)
        acc_ref[...] = jnp.zeros(acc_ref.shape, F32)

    qrows = qrows_ref[...]

    def bias_rows(s, idx):
        keep = (s > thr) | ((s == thr) & (idx <= j_max))
        return jnp.concatenate([jnp.where(keep, 0.0, NEG_BIG)] * N_HEADS, axis=0)

    def flat(page):
        heads = [page[pl.ds(h, PAGE_SIZE, stride=N_HEADS), :] for h in range(N_HEADS)]
        return jnp.concatenate(heads, axis=1).astype(BF16)

    def update(s, v_list):
        m_old = m_ref[...]
        m_new = jnp.maximum(m_old, jnp.max(s, axis=1, keepdims=True))
        alpha = jnp.exp(m_old - m_new)
        p = jnp.exp(s - m_new)
        l_ref[...] = alpha * l_ref[...] + jnp.sum(p, axis=1, keepdims=True)
        pv = jnp.zeros(acc_ref.shape, F32)
        for r, v in enumerate(v_list):
            pv = pv + jnp.dot(p[:, r * LANES:(r + 1) * LANES].astype(BF16), v, preferred_element_type=F32)
        acc_ref[...] = alpha * acc_ref[...] + pv
        m_ref[...] = m_new

    width = ATTN_PAGES * PAGE_SIZE
    idx = j * width + lax.broadcasted_iota(I32, (n_q, width), 1)
    s = jnp.concatenate([_dot_t(qrows, flat(kp)) for kp in k_pages], axis=1) + bias_rows(sp_ref[...], idx)
    update(s, [flat(vp) for vp in v_pages])

    @pl.when(j == pl.num_programs(1) - 1)
    def _():
        idx_n = past + lax.broadcasted_iota(I32, (n_q, LANES), 1)
        update(_dot_t(qrows, kn_ref[...]) + bias_rows(sn_ref[...], idx_n), [vn_ref[...]])
        out = acc_ref[...] / l_ref[...]
        for h in range(N_HEADS):
            cs = slice(h * HEAD_DIM, (h + 1) * HEAD_DIM)
            o_ref[:, cs] = out[h * n_q:(h + 1) * n_q, cs] * ga_ref[:, cs]


def _sample_attention(page_table, qrows, scores_past, scores_new, thr, j_max, k_new_pad, v_new_pad, ga,
                      cache_k, cache_v, layer):
    db, n_pages = page_table.shape
    n_q = scores_past.shape[1]
    n_rows = qrows.shape[1]
    steps = n_pages // ATTN_PAGES
    per_b = lambda *shape: pl.BlockSpec((None,) + shape, lambda b, j, pt: (b, 0, 0))
    page_spec = lambda r: pl.BlockSpec((None, None, PAGE_SIZE * N_HEADS, HEAD_DIM),
                                       lambda b, j, pt, r=r: (layer, pt[b, j * ATTN_PAGES + r], 0, 0))
    grid_spec = pltpu.PrefetchScalarGridSpec(
        num_scalar_prefetch=1,
        grid=(db, steps),
        in_specs=[per_b(n_rows, D_ATTN),
                  pl.BlockSpec((None, n_q, ATTN_PAGES * PAGE_SIZE), lambda b, j, pt: (b, 0, j)),
                  per_b(n_q, LANES), per_b(n_q, LANES), per_b(n_q, LANES),
                  per_b(LANES, D_ATTN), per_b(LANES, D_ATTN), per_b(n_q, D_ATTN)]
                 + [page_spec(r) for r in range(ATTN_PAGES)] * 2,
        out_specs=per_b(n_q, D_ATTN),
        scratch_shapes=[pltpu.VMEM((n_rows, 1), F32), pltpu.VMEM((n_rows, 1), F32),
                        pltpu.VMEM((n_rows, D_ATTN), F32)],
    )
    return pl.pallas_call(
        functools.partial(_sample_attn_kernel, n_q=n_q, past=n_pages * PAGE_SIZE),
        out_shape=jax.ShapeDtypeStruct((db, n_q, D_ATTN), F32),
        grid_spec=grid_spec,
        compiler_params=_params("parallel", "arbitrary"),
        name="sample_attn",
    )(page_table, qrows, scores_past, scores_new, thr, j_max, k_new_pad, v_new_pad, ga,
      *([cache_k] * ATTN_PAGES), *([cache_v] * ATTN_PAGES))


CONV_TC = 256
CONV_HALO = 32
CONV_RB = 32


def _conv_finish(y, cb, g, b, gate):
    y = _layer_norm(y + cb, g, b)
    return y * _sigmoid(y) * gate


def _conv_prompt_kernel(u_ref, prev_ref, head_ref, gc_ref, w_ref, cb_ref, g_ref, b_ref, o_ref, win_ref):
    i = pl.program_id(1)

    @pl.when(i == 0)
    def _():
        win_ref[0:CONV_HALO, :] = head_ref[...]

    @pl.when(i > 0)
    def _():
        win_ref[0:CONV_HALO, :] = prev_ref[...]

    win_ref[CONV_HALO:, :] = u_ref[...]
    off = CONV_HALO - (CONV_W - 1)
    for r0 in range(0, CONV_TC, CONV_RB):
        acc = jnp.zeros((CONV_RB, u_ref.shape[1]), F32)
        for t in range(CONV_W):
            acc = acc + win_ref[r0 + off + t:r0 + off + t + CONV_RB, :] * w_ref[t:t + 1, :]
        gate = gc_ref[r0:r0 + CONV_RB, :].astype(F32)
        o_ref[r0:r0 + CONV_RB, :] = _conv_finish(acc, cb_ref[...], g_ref[...], b_ref[...], gate).astype(o_ref.dtype)


def _conv_prompt(u, head, gc, conv_w, conv_b, g, b, n_batch, seq):
    c = u.shape[-1]
    u3 = u.reshape(n_batch, seq, c)
    gc3 = gc.reshape(n_batch, seq, c)
    ratio = CONV_TC // CONV_HALO
    vec = lambda: pl.BlockSpec((1, c), lambda bb, i: (0, 0))
    out = pl.pallas_call(
        _conv_prompt_kernel,
        out_shape=jax.ShapeDtypeStruct((n_batch, seq, c), BF16),
        grid=(n_batch, seq // CONV_TC),
        in_specs=[pl.BlockSpec((None, CONV_TC, c), lambda bb, i: (bb, i, 0)),
                  pl.BlockSpec((None, CONV_HALO, c), lambda bb, i: (bb, jnp.maximum(i * ratio - 1, 0), 0)),
                  pl.BlockSpec((CONV_HALO, c), lambda bb, i: (0, 0)),
                  pl.BlockSpec((None, CONV_TC, c), lambda bb, i: (bb, i, 0)),
                  pl.BlockSpec((CONV_W, c), lambda bb, i: (0, 0)),
                  vec(), vec(), vec()],
        out_specs=pl.BlockSpec((None, CONV_TC, c), lambda bb, i: (bb, i, 0)),
        scratch_shapes=[pltpu.VMEM((CONV_HALO + CONV_TC, c), F32)],
        compiler_params=_params("parallel", "arbitrary"),
        name="conv_prompt",
    )(u3, u3, head, gc3, conv_w, conv_b.reshape(1, c), g.reshape(1, c), b.reshape(1, c))
    return out.reshape(n_batch * seq, c)


def _conv_sample_kernel(st_ref, u_ref, gc_ref, w_ref, cb_ref, g_ref, b_ref, o_ref):
    n_hist = st_ref.shape[0]
    n_new = u_ref.shape[0]
    for r in range(n_new):
        acc = jnp.zeros(u_ref.shape[1:], F32)
        for t in range(CONV_W):
            src = r + t
            row = st_ref[src] if src < n_hist else u_ref[src - n_hist]
            acc = acc + row * w_ref[t:t + 1, :]
        o_ref[r] = _conv_finish(acc, cb_ref[...], g_ref[...], b_ref[...], gc_ref[r])


def _conv_sample(state_t, u_t, gc_t, conv_w, conv_b, g, b):
    c = u_t.shape[-1]
    return pl.pallas_call(
        _conv_sample_kernel,
        out_shape=jax.ShapeDtypeStruct(u_t.shape, F32),
        compiler_params=pltpu.CompilerParams(vmem_limit_bytes=VMEM_LIMIT),
        name="conv_sample",
    )(state_t, u_t, gc_t, conv_w, conv_b.reshape(1, c), g.reshape(1, c), b.reshape(1, c))


def _out_kernel(x_ref, a_ref, c_ref, wa_ref, wc_ref, gi_ref, bi_ref, go_ref, bo_ref, o_ref, *, alpha):
    h = _layer_norm(x_ref[...], gi_ref[...], bi_ref[...])
    z = jnp.dot(a_ref[...], wa_ref[...], preferred_element_type=F32)
    z = z + jnp.dot(c_ref[...], wc_ref[...], preferred_element_type=F32)
    o_ref[...] = _layer_norm(alpha * h + z, go_ref[...], bo_ref[...])


def _out_projection(x, a, c, wa, wc, gi, bi, go, bo, alpha, tm):
    m, d = x.shape
    row = lambda w: pl.BlockSpec((tm, w), lambda i: (i, 0))
    full = lambda arr: pl.BlockSpec(arr.shape, lambda i: (0, 0))
    vecs = [v.reshape(1, d) for v in (gi, bi, go, bo)]
    return pl.pallas_call(
        functools.partial(_out_kernel, alpha=alpha),
        out_shape=jax.ShapeDtypeStruct((m, d), F32),
        grid=(m // tm,),
        in_specs=[row(d), row(a.shape[1]), row(c.shape[1]), full(wa), full(wc)] + [full(v) for v in vecs],
        out_specs=row(d),
        compiler_params=_params("parallel"),
        name="out_proj",
    )(x, a, c, wa, wc, *vecs)


def _split_w_in(w):
    d_conv = (w.shape[1] - (3 * D_ATTN + N_IDX_HEADS * D_IDX + D_IDX + N_IDX_HEADS + D_ATTN)) // 3
    o = [0]
    for n in (D_ATTN, 2 * D_ATTN, N_IDX_HEADS * D_IDX, D_IDX + N_IDX_HEADS, D_ATTN, d_conv, d_conv, d_conv):
        o.append(o[-1] + n)
    piece = lambda i: w[:, o[i]:o[i + 1]].astype(BF16)
    kiwi = jnp.pad(w[:, o[3]:o[4]], ((0, 0), (0, 2 * LANES - (D_IDX + N_IDX_HEADS)))).astype(BF16)
    return dict(q=piece(0), kv=piece(1), qi=piece(2), kiwi=kiwi, ga=piece(4),
                glu_a=piece(5), glu_b=piece(6), gc=piece(7))


def kernel(x_prompt, x_sample, cache_k, cache_v, cache_kidx, state_conv, page_table, meta_tokens,
           ln_in_g, ln_in_b, w_in, ln_kidx_g, ln_kidx_b, conv_w, conv_b, ln_conv_g, ln_conv_b,
           w_out, ln_out_g, ln_out_b):
    n_batch, seq, d_model = x_prompt.shape
    db, n_new, _ = x_sample.shape
    depth = w_in.shape[0]
    assert depth == 1, "one mixer layer per step"
    assert seq >= CONV_W - 1
    n_pages = page_table.shape[1]
    past = n_pages * PAGE_SIZE
    n_sel_prompt = min(TOPK_MAX, seq // 4)
    n_sel_sample = min(TOPK_MAX, (past + n_new) // 4)
    alpha = (2.0 * depth) ** 0.25
    l = 0

    w = _split_w_in(w_in[l])
    d_conv = w["gc"].shape[1]
    wo = w_out[l].astype(BF16)
    wo_a, wo_c = wo[:D_ATTN], wo[D_ATTN:]

    xp = x_prompt.reshape(n_batch * seq, d_model)
    xs = jnp.concatenate([meta_tokens.astype(F32), x_sample.reshape(db * n_new, d_model)], axis=0)
    n_small = xs.shape[0]
    hp = _ln_cast(xp, ln_in_g, ln_in_b, 512)
    hs = _ln_cast(xs, ln_in_g, ln_in_b, n_small)
    pp = _in_projection(hp, w, ln_kidx_g[l], ln_kidx_b[l], 512)
    ps = _in_projection(hs, w, ln_kidx_g[l], ln_kidx_b[l], n_small)
    pm = {n: a[:N_META] for n, a in ps.items()}
    ps = {n: a[N_META:] for n, a in ps.items()}

    pad_rows = lambda a: jnp.pad(a, ((0, LANES - a.shape[0]), (0, 0)))
    meta = dict(ki=pad_rows(pm["ki16"]), k=pad_rows(pm["kv16"][:, :D_ATTN]),
                vt=jnp.transpose(pad_rows(pm["kv16"][:, D_ATTN:])))
    a_p = _prompt_attention(pp, meta, n_batch, seq, n_sel_prompt).reshape(n_batch * seq, D_ATTN)
    head = jnp.concatenate([jnp.zeros((CONV_HALO - N_META, d_conv), F32), pm["u"]], axis=0)
    c_p = _conv_prompt(pp["u"], head, pp["gc"], conv_w[l], conv_b[l], ln_conv_g[l], ln_conv_b[l], n_batch, seq)
    y_p = _out_projection(xp, a_p, c_p, wo_a, wo_c, ln_in_g, ln_in_b, ln_out_g[l], ln_out_b[l], alpha, 256)
    y_prompt = y_p.reshape(n_batch, seq, d_model)

    def with_meta(m_rows, p_rows):
        m_b = jnp.broadcast_to(m_rows[None], (n_batch,) + m_rows.shape)
        return jnp.concatenate([m_b, p_rows.reshape(n_batch, seq, -1)], axis=1)

    new_k_p = with_meta(pm["kv32"][:, :D_ATTN], pp["kv32"][:, :D_ATTN]).reshape(1, n_batch, N_META + seq, N_HEADS, HEAD_DIM)
    new_v_p = with_meta(pm["kv32"][:, D_ATTN:], pp["kv32"][:, D_ATTN:]).reshape(1, n_batch, N_META + seq, N_HEADS, HEAD_DIM)
    new_ki_p = with_meta(pm["ki32"], pp["ki32"])[None]
    new_conv_p = pp["u"].reshape(n_batch, seq, d_conv)[:, -(CONV_W - 1):][None]

    qi_rows = ps["qi"].reshape(db, n_new * N_IDX_HEADS, D_IDX)
    wcol = jnp.broadcast_to(ps["wi"][:, :N_IDX_HEADS].reshape(db, n_new * N_IDX_HEADS, 1),
                            (db, n_new * N_IDX_HEADS, LANES))
    pad_new = lambda a: jnp.pad(a.reshape(db, n_new, a.shape[-1]), ((0, 0), (0, LANES - n_new), (0, 0)))
    sc_past, sc_new = _sample_scores(page_table, qi_rows, wcol, pad_new(ps["ki16"]), cache_kidx, l)
    thr, j_max = _sample_threshold(sc_past, sc_new, n_sel_sample)
    q4 = ps["q"].reshape(db, n_new, N_HEADS, HEAD_DIM)
    eye = jnp.eye(N_HEADS, dtype=BF16)
    qrows = jnp.einsum("bqhd,hg->bhqgd", q4, eye).reshape(db, N_HEADS * n_new, D_ATTN)
    ga_s = ps["ga"].astype(F32).reshape(db, n_new, D_ATTN)
    page_rows = lambda c: c.reshape(c.shape[0], c.shape[1], PAGE_SIZE * N_HEADS, HEAD_DIM)
    a_s = _sample_attention(page_table, qrows, sc_past, sc_new, thr, j_max,
                            pad_new(ps["kv16"][:, :D_ATTN]), pad_new(ps["kv16"][:, D_ATTN:]), ga_s,
                            page_rows(cache_k), page_rows(cache_v), l)
    a_s = a_s.reshape(db * n_new, D_ATTN).astype(BF16)

    to_t = lambda a: jnp.transpose(a.reshape(db, -1, d_conv), (1, 0, 2))
    c_t = _conv_sample(to_t(state_conv[l].astype(F32)), to_t(ps["u"]), to_t(ps["gc"].astype(F32)),
                       conv_w[l], conv_b[l], ln_conv_g[l], ln_conv_b[l])
    c_s = jnp.transpose(c_t, (1, 0, 2)).reshape(db * n_new, d_conv).astype(BF16)
    y_s = _out_projection(x_sample.reshape(db * n_new, d_model), a_s, c_s, wo_a, wo_c,
                          ln_in_g, ln_in_b, ln_out_g[l], ln_out_b[l], alpha, db * n_new)
    y_sample = y_s.reshape(db, n_new, d_model)

    new_k_s = ps["kv32"][:, :D_ATTN].reshape(1, db, n_new, N_HEADS, HEAD_DIM)
    new_v_s = ps["kv32"][:, D_ATTN:].reshape(1, db, n_new, N_HEADS, HEAD_DIM)
    new_ki_s = ps["ki32"].reshape(1, db, n_new, D_IDX)
    u_ext_s = jnp.concatenate([state_conv[l].astype(F32), ps["u"].reshape(db, n_new, d_conv)], axis=1)
    new_conv_s = u_ext_s[:, -(CONV_W - 1):][None]

    return (y_prompt, y_sample, new_k_p, new_v_p, new_ki_p, new_conv_p,
            new_k_s, new_v_s, new_ki_s, new_conv_s)
```

```python
import functools

import jax
import jax.numpy as jnp
from jax import lax
from jax.experimental import pallas as pl
from jax.experimental.pallas import tpu as pltpu
from jax.experimental.pallas import tpu_sc as plsc

N_META = 16
N_HEADS = 8
HEAD_DIM = 128
D_ATTN = N_HEADS * HEAD_DIM
N_IDX_HEADS = 16
D_IDX = 128
TOPK_MAX = 256
CONV_W = 31
PAGE_SIZE = 128
LN_EPS = 1e-5
ATTN_SCALE = HEAD_DIM ** -0.5
IDX_SCALE = (N_IDX_HEADS * D_IDX) ** -0.5

LANES = 128
SUBLANES = 8
NEG_BIG = -1e30
F32_MAX = 3.4028235e38
INT_MAX = 2 ** 31 - 1
SEARCH_CAP = 400
VMEM_LIMIT = 56 * 1024 * 1024

F32 = jnp.float32
BF16 = jnp.bfloat16
I32 = jnp.int32


def _params(*sem):
    return pltpu.CompilerParams(dimension_semantics=sem, vmem_limit_bytes=VMEM_LIMIT)


def _layer_norm(x, g, b):
    mu = jnp.mean(x, axis=-1, keepdims=True)
    xc = x - mu
    var = jnp.mean(xc * xc, axis=-1, keepdims=True)
    return xc * lax.rsqrt(var + LN_EPS) * g + b


def _sigmoid(z):
    return 1.0 / (1.0 + jnp.exp(-z))


def _dot_t(a, b):
    return lax.dot_general(a, b, (((1,), (1,)), ((), ())), preferred_element_type=F32)


def _fold_rows(x, reduce):
    return reduce(x.reshape(x.shape[0] // SUBLANES, SUBLANES, x.shape[1]), axis=0)


def _col_reduce(x, reduce):
    return reduce(_fold_rows(x, reduce), axis=0, keepdims=True)


def _topk_threshold(count, lo0, hi0, c0, active, k, n_total):
    one = jnp.ones(lo0.shape, I32)

    def cond(st):
        it, _, _, c_lo, stalled = st
        waiting = jnp.sum(jnp.where(active & (c_lo != k) & (stalled == 0), one, 0))
        return (it < SEARCH_CAP) & (waiting > 0)

    def body(st):
        it, lo, hi, c_lo, stalled = st
        mid = 0.5 * lo + 0.5 * hi
        stall = (mid <= lo) | (mid >= hi)
        c = count(lambda s, i: s >= mid)
        up = (c >= k) & jnp.logical_not(stall)
        down = (c < k) & jnp.logical_not(stall)
        return (it + 1, jnp.where(up, mid, lo), jnp.where(down, mid, hi), jnp.where(up, c, c_lo),
                jnp.where(stall, one, stalled))

    _, thr, _, c_lo, _ = lax.while_loop(cond, body, (jnp.int32(0), lo0, hi0, c0, jnp.zeros(lo0.shape, I32)))
    tied = active & (c_lo > k)

    def break_ties(_):
        need = k - count(lambda s, i: s > thr)

        def step(_, st):
            j_lo, j_hi = st
            j_mid = j_lo + (j_hi - j_lo) // 2
            ok = count(lambda s, i: (s == thr) & (i <= j_mid)) >= need
            return jnp.where(ok, j_lo, j_mid), jnp.where(ok, j_mid, j_hi)

        n_steps = (n_total + 1).bit_length()
        _, j_hi = lax.fori_loop(0, n_steps, step, (jnp.full(lo0.shape, -1, I32), jnp.full(lo0.shape, n_total, I32)))
        return jnp.where(tied, j_hi, INT_MAX)

    j_max = lax.cond(jnp.sum(jnp.where(tied, one, 0)) > 0, break_ties,
                     lambda _: jnp.full(lo0.shape, INT_MAX, I32), 0)
    thr = jnp.where(active, thr, -F32_MAX)
    return thr, j_max


def _above(x):
    return x + (jnp.abs(x) * 2.0 ** -20 + 1e-30)


def _ln_kernel(x_ref, g_ref, b_ref, o_ref):
    o_ref[...] = _layer_norm(x_ref[...], g_ref[...], b_ref[...]).astype(o_ref.dtype)


def _ln_cast(x, g, b, tm):
    m, d = x.shape
    return pl.pallas_call(
        _ln_kernel,
        out_shape=jax.ShapeDtypeStruct((m, d), BF16),
        grid=(m // tm,),
        in_specs=[pl.BlockSpec((tm, d), lambda i: (i, 0)),
                  pl.BlockSpec((1, d), lambda i: (0, 0)),
                  pl.BlockSpec((1, d), lambda i: (0, 0))],
        out_specs=pl.BlockSpec((tm, d), lambda i: (i, 0)),
        compiler_params=_params("parallel"),
        name="ln_in",
    )(x, g.reshape(1, d), b.reshape(1, d))


def _proj_q_kernel(h_ref, w_ref, o_ref):
    z = jnp.dot(h_ref[...], w_ref[...], preferred_element_type=F32)
    o_ref[...] = (z * ATTN_SCALE).astype(o_ref.dtype)


def _proj_cast_kernel(h_ref, w_ref, o_ref):
    o_ref[...] = jnp.dot(h_ref[...], w_ref[...], preferred_element_type=F32).astype(o_ref.dtype)


def _proj_kv_kernel(h_ref, w_ref, o32_ref, o16_ref):
    z = jnp.dot(h_ref[...], w_ref[...], preferred_element_type=F32)
    o32_ref[...] = z
    o16_ref[...] = z.astype(o16_ref.dtype)


def _proj_silu_kernel(h_ref, w_ref, o_ref):
    z = jnp.dot(h_ref[...], w_ref[...], preferred_element_type=F32)
    o_ref[...] = (z * _sigmoid(z)).astype(o_ref.dtype)


def _proj_glu_kernel(h_ref, wa_ref, wb_ref, o_ref):
    h = h_ref[...]
    za = jnp.dot(h, wa_ref[...], preferred_element_type=F32)
    zb = jnp.dot(h, wb_ref[...], preferred_element_type=F32)
    o_ref[...] = za * _sigmoid(zb)


def _proj_kiwi_kernel(h_ref, w_ref, g_ref, b_ref, ki32_ref, ki16_ref, wi_ref):
    z = jnp.dot(h_ref[...], w_ref[...], preferred_element_type=F32)
    ki = _layer_norm(z[:, :D_IDX], g_ref[...], b_ref[...])
    ki32_ref[...] = ki
    ki16_ref[...] = ki.astype(ki16_ref.dtype)
    wi_ref[...] = z[:, D_IDX:] * IDX_SCALE


def _proj(kernel, h, ws, extra, outs, tm, name):
    m, d = h.shape
    in_specs = [pl.BlockSpec((tm, d), lambda i: (i, 0))]
    in_specs += [pl.BlockSpec(w.shape, lambda i: (0, 0)) for w in ws]
    in_specs += [pl.BlockSpec(e.shape, lambda i: (0, 0)) for e in extra]
    out_shape = [jax.ShapeDtypeStruct((m, n), dt) for n, dt in outs]
    out_specs = [pl.BlockSpec((tm, n), lambda i: (i, 0)) for n, _ in outs]
    return pl.pallas_call(
        kernel,
        out_shape=out_shape,
        grid=(m // tm,),
        in_specs=in_specs,
        out_specs=out_specs,
        compiler_params=_params("parallel"),
        name=name,
    )(h, *ws, *extra)


def _in_projection(h, w, kidx_g, kidx_b, tm):
    (q,) = _proj(_proj_q_kernel, h, [w["q"]], [], [(D_ATTN, BF16)], tm, "proj_q")
    kv32, kv16 = _proj(_proj_kv_kernel, h, [w["kv"]], [], [(2 * D_ATTN, F32), (2 * D_ATTN, BF16)], tm, "proj_kv")
    (qi,) = _proj(_proj_cast_kernel, h, [w["qi"]], [], [(N_IDX_HEADS * D_IDX, BF16)], tm, "proj_qi")
    ki32, ki16, wi = _proj(_proj_kiwi_kernel, h, [w["kiwi"]], [kidx_g.reshape(1, D_IDX), kidx_b.reshape(1, D_IDX)],
                           [(D_IDX, F32), (D_IDX, BF16), (LANES, F32)], tm, "proj_kiwi")
    (ga,) = _proj(_proj_silu_kernel, h, [w["ga"]], [], [(D_ATTN, BF16)], tm, "proj_gattn")
    (u,) = _proj(_proj_glu_kernel, h, [w["glu_a"], w["glu_b"]], [], [(w["glu_a"].shape[1], F32)], tm, "proj_glu")
    (gc,) = _proj(_proj_silu_kernel, h, [w["gc"]], [], [(w["gc"].shape[1], BF16)], tm, "proj_gconv")
    return dict(q=q, kv32=kv32, kv16=kv16, qi=qi, ki32=ki32, ki16=ki16, wi=wi, ga=ga, u=u, gc=gc)


TQ = 256
KC = 256


def _prompt_attn_kernel(qi_ref, wi_ref, q_ref, ga_ref, kip_ref, kp_ref, vtp_ref, kim_ref, km_ref, vtm_ref,
                        o_ref, sc_ref, scm_ref, wt_ref, s_ref, m_ref, l_ref, acc_ref, *, n_sel):
    qb = pl.program_id(1)
    n_chunks = (qb * TQ + TQ - 1) // KC + 1
    q_pos = qb * TQ + lax.broadcasted_iota(I32, (1, TQ), 1)
    row_kc = lax.broadcasted_iota(I32, (KC, TQ), 0)
    row_m = lax.broadcasted_iota(I32, (LANES, TQ), 0)

    wt_ref[...] = jnp.transpose(wi_ref[...])

    def idx_scores(ki):
        acc = jnp.zeros((ki.shape[0], TQ), F32)
        for h in range(N_IDX_HEADS):
            s = _dot_t(ki, qi_ref[:, h * D_IDX:(h + 1) * D_IDX])
            acc = acc + jnp.maximum(s, 0.0) * wt_ref[h:h + 1, :]
        return acc

    sm = idx_scores(kim_ref[...])
    meta_vis = row_m < N_META
    scm_ref[...] = jnp.where(meta_vis, sm, -jnp.inf)

    def score_chunk(j, carry):
        rmax, rmin = carry
        r0 = pl.multiple_of(j * KC, KC)
        s = idx_scores(kip_ref[pl.ds(r0, KC), :])
        vis = (j * KC + row_kc) <= q_pos
        s_vis = jnp.where(vis, s, -jnp.inf)
        sc_ref[j] = s_vis
        rmax = jnp.maximum(rmax, _fold_rows(s_vis, jnp.max))
        rmin = jnp.minimum(rmin, _fold_rows(jnp.where(vis, s, jnp.inf), jnp.min))
        return rmax, rmin

    rmax, rmin = lax.fori_loop(0, n_chunks, score_chunk,
                               (_fold_rows(jnp.where(meta_vis, sm, -jnp.inf), jnp.max),
                                _fold_rows(jnp.where(meta_vis, sm, jnp.inf), jnp.min)))

    def count(f):
        def ones(mask):
            return _fold_rows(jnp.where(mask, 1, 0), jnp.sum)

        def body(j, c):
            return c + ones(f(sc_ref[j], LANES + j * KC + row_kc))

        c = lax.fori_loop(0, n_chunks, body, ones(f(scm_ref[...], row_m)))
        return jnp.sum(c, axis=0, keepdims=True)

    n_vis = N_META + 1 + q_pos
    thr, j_max = _topk_threshold(count, jnp.min(rmin, axis=0, keepdims=True),
                                 _above(jnp.max(rmax, axis=0, keepdims=True)), n_vis,
                                 n_vis > n_sel, n_sel, LANES + sc_ref.shape[0] * KC)

    def bias_of(s, idx):
        keep = (s > thr) | ((s == thr) & (idx <= j_max))
        return jnp.where(keep, 0.0, NEG_BIG)

    m_ref[...] = jnp.full(m_ref.shape, NEG_BIG, F32)
    l_ref[...] = jnp.zeros(l_ref.shape, F32)
    acc_ref[...] = jnp.zeros(acc_ref.shape, F32)

    def attend(bias, k_blk, vt_blk):
        n = bias.shape[0]
        heads = [slice(h * HEAD_DIM, (h + 1) * HEAD_DIM) for h in range(N_HEADS)]
        m_new = []
        for h, cs in enumerate(heads):
            s = _dot_t(k_blk[:, cs], q_ref[:, cs]) + bias
            s_ref[h, 0:n, :] = s
            m_new.append(jnp.maximum(m_ref[h], _col_reduce(s, jnp.max)))
        for h, cs in enumerate(heads):
            alpha = jnp.exp(m_ref[h] - m_new[h])
            p = jnp.exp(s_ref[h, 0:n, :] - m_new[h])
            l_ref[h] = alpha * l_ref[h] + _col_reduce(p, jnp.sum)
            pv = jnp.dot(vt_blk[cs, :], p.astype(BF16), preferred_element_type=F32)
            acc_ref[cs, :] = alpha * acc_ref[cs, :] + pv
            m_ref[h] = m_new[h]

    attend(bias_of(scm_ref[...], row_m), km_ref[...], vtm_ref[...])

    def attn_chunk(j, carry):
        r0 = pl.multiple_of(j * KC, KC)
        attend(bias_of(sc_ref[j], LANES + j * KC + row_kc), kp_ref[pl.ds(r0, KC), :], vtp_ref[j])
        return carry

    lax.fori_loop(0, n_chunks, attn_chunk, 0)

    for h in range(N_HEADS):
        cs = slice(h * HEAD_DIM, (h + 1) * HEAD_DIM)
        out = jnp.transpose(acc_ref[cs, :] / l_ref[h])
        o_ref[:, cs] = (out * ga_ref[:, cs].astype(F32)).astype(o_ref.dtype)


def _prompt_attention(p, meta, n_batch, seq, n_sel):
    def rows(a):
        return a.reshape(n_batch, seq, a.shape[-1])

    qi, wi, q, ga, ki16, kv16 = (rows(p[n]) for n in ("qi", "wi", "q", "ga", "ki16", "kv16"))
    n_kc = seq // KC
    vt = jnp.transpose(kv16[:, :, D_ATTN:].reshape(n_batch, n_kc, KC, D_ATTN), (0, 1, 3, 2))
    blk = lambda w: pl.BlockSpec((None, TQ, w), lambda b, i: (b, i, 0))
    res = lambda w: pl.BlockSpec((None, seq, w), lambda b, i: (b, 0, 0), pipeline_mode=pl.Buffered(1))
    const = lambda a: pl.BlockSpec(a.shape, lambda b, i: (0, 0), pipeline_mode=pl.Buffered(1))
    return pl.pallas_call(
        functools.partial(_prompt_attn_kernel, n_sel=n_sel),
        out_shape=jax.ShapeDtypeStruct((n_batch, seq, D_ATTN), BF16),
        grid=(n_batch, seq // TQ),
        in_specs=[blk(N_IDX_HEADS * D_IDX), blk(LANES), blk(D_ATTN), blk(D_ATTN),
                  res(D_IDX), res(D_ATTN),
                  pl.BlockSpec((None, n_kc, D_ATTN, KC), lambda b, i: (b, 0, 0, 0), pipeline_mode=pl.Buffered(1)),
                  const(meta["ki"]), const(meta["k"]), const(meta["vt"])],
        out_specs=blk(D_ATTN),
        scratch_shapes=[pltpu.VMEM((n_kc, KC, TQ), F32),
                        pltpu.VMEM((LANES, TQ), F32),
                        pltpu.VMEM((LANES, TQ), F32),
                        pltpu.VMEM((N_HEADS, KC, TQ), F32),
                        pltpu.VMEM((N_HEADS, 1, TQ), F32),
                        pltpu.VMEM((N_HEADS, 1, TQ), F32),
                        pltpu.VMEM((D_ATTN, TQ), F32)],
        compiler_params=_params("parallel", "arbitrary"),
        name="prompt_attn",
    )(qi, wi, q, ga, ki16, kv16, vt, meta["ki"], meta["k"], meta["vt"])


SCORE_PAGES = 32


def _idx_rows(qi, wcol, ki, n_q):
    a = jnp.maximum(_dot_t(qi, ki), 0.0) * wcol
    return jnp.sum(a.reshape(n_q, N_IDX_HEADS, ki.shape[0]), axis=1)


def _sample_score_kernel(pt_ref, qi_ref, wcol_ref, kin_ref, *refs):
    pages, (o_ref, on_ref) = refs[:SCORE_PAGES], refs[SCORE_PAGES:]
    n_q = o_ref.shape[0]
    qi = qi_ref[...]
    wcol = wcol_ref[...]
    for r, page in enumerate(pages):
        o_ref[:, r * PAGE_SIZE:(r + 1) * PAGE_SIZE] = _idx_rows(qi, wcol, page[...].astype(BF16), n_q)

    @pl.when(pl.program_id(1) == pl.num_programs(1) - 1)
    def _():
        sc = _idx_rows(qi, wcol, kin_ref[...], n_q)
        lane = lax.broadcasted_iota(I32, (n_q, LANES), 1)
        row = lax.broadcasted_iota(I32, (n_q, LANES), 0)
        on_ref[...] = jnp.where(lane <= row, sc, -jnp.inf)


def _sample_scores(page_table, qi_rows, wcol, ki_new_pad, cache_kidx, layer):
    db, n_pages = page_table.shape
    n_rows = qi_rows.shape[1]
    n_q = n_rows // N_IDX_HEADS
    steps = n_pages // SCORE_PAGES
    per_b = lambda *shape: pl.BlockSpec((None,) + shape, lambda b, j, pt: (b, 0, 0))
    page_spec = lambda r: pl.BlockSpec((None, None, PAGE_SIZE, D_IDX),
                                       lambda b, j, pt, r=r: (layer, pt[b, j * SCORE_PAGES + r], 0, 0))
    grid_spec = pltpu.PrefetchScalarGridSpec(
        num_scalar_prefetch=1,
        grid=(db, steps),
        in_specs=[per_b(n_rows, D_IDX), per_b(n_rows, LANES), per_b(LANES, D_IDX)]
                 + [page_spec(r) for r in range(SCORE_PAGES)],
        out_specs=[pl.BlockSpec((None, n_q, SCORE_PAGES * PAGE_SIZE), lambda b, j, pt: (b, 0, j)),
                   per_b(n_q, LANES)],
    )
    return pl.pallas_call(
        _sample_score_kernel,
        out_shape=[jax.ShapeDtypeStruct((db, n_q, n_pages * PAGE_SIZE), F32),
                   jax.ShapeDtypeStruct((db, n_q, LANES), F32)],
        grid_spec=grid_spec,
        compiler_params=_params("parallel", "arbitrary"),
        name="sample_scores",
    )(page_table, qi_rows, wcol, ki_new_pad, *([cache_kidx] * SCORE_PAGES))


def _sample_thr_kernel(sp_ref, sn_ref, thr_ref, j_ref, *, n_sel, n_q):
    rows, past = sp_ref.shape
    lane = lax.broadcasted_iota(I32, (rows, LANES), 1)
    idx_p = lax.broadcasted_iota(I32, (rows, past), 1)
    sn = sn_ref[...]

    def count(f):
        c = jnp.sum(jnp.where(f(sp_ref[...], idx_p), 1, 0), axis=1, keepdims=True)
        return c + jnp.sum(jnp.where(f(sn, past + lane), 1, 0), axis=1, keepdims=True)

    new_vis = sn > -jnp.inf
    rmax = jnp.maximum(jnp.max(sp_ref[...], axis=1, keepdims=True), jnp.max(sn, axis=1, keepdims=True))
    rmin = jnp.minimum(jnp.min(sp_ref[...], axis=1, keepdims=True),
                       jnp.min(jnp.where(new_vis, sn, jnp.inf), axis=1, keepdims=True))
    n_vis = past + 1 + lax.rem(lax.broadcasted_iota(I32, (rows, 1), 0), n_q)
    thr, j_max = _topk_threshold(count, rmin, _above(rmax), n_vis, n_vis > n_sel, n_sel, past + LANES)
    thr_ref[...] = jnp.broadcast_to(thr, thr_ref.shape)
    j_ref[...] = jnp.broadcast_to(j_max, j_ref.shape)


def _sample_threshold(scores_past, scores_new, n_sel):
    db, n_q, past = scores_past.shape
    rows = db * n_q
    out = jax.ShapeDtypeStruct((rows, LANES), F32), jax.ShapeDtypeStruct((rows, LANES), I32)
    thr, j_max = pl.pallas_call(
        functools.partial(_sample_thr_kernel, n_sel=n_sel, n_q=n_q),
        out_shape=out,
        compiler_params=pltpu.CompilerParams(vmem_limit_bytes=VMEM_LIMIT),
        name="sample_threshold",
    )(scores_past.reshape(rows, past), scores_new.reshape(rows, LANES))
    return thr.reshape(db, n_q, LANES), j_max.reshape(db, n_q, LANES)


SC_LANES = 16
SC_WORKERS = 32
GATHER_ROWS = 32


def _sc_gather_kernel(sc_hbm, thr_hbm, jmax_hbm, pt_hbm, kc_hbm, vc_hbm, ksel_hbm, vsel_hbm, cnt_hbm,
                      sc_v, thr_v, jmax_v, pt_v, list_v, cnt_v, buf0, buf1, sem0, sem1,
                      *, n_q, past, n_pages, n_sel, rows_per_worker, n_cores):
    wid = lax.axis_index("s") * n_cores + lax.axis_index("c")
    lanes = lax.iota(I32, SC_LANES)
    zero = jnp.zeros((SC_LANES,), I32)
    bufs, sems = (buf0, buf1), (sem0, sem1)

    @pl.loop(0, rows_per_worker)
    def _(t):
        r = wid * rows_per_worker + t
        pltpu.sync_copy(pt_hbm.at[pl.ds((r // n_q) * n_pages, n_pages)], pt_v)
        pltpu.sync_copy(sc_hbm.at[pl.ds(r * past, past)], sc_v)
        pltpu.sync_copy(thr_hbm.at[pl.ds(r * LANES, SC_LANES)], thr_v)
        pltpu.sync_copy(jmax_hbm.at[pl.ds(r * LANES, SC_LANES)], jmax_v)
        thr = thr_v[...]
        j_max = jmax_v[...]
        for i in range(n_sel // SC_LANES):
            list_v[pl.ds(i * SC_LANES, SC_LANES)] = zero

        def body(i, cnt):
            base = i * SC_LANES
            s = sc_v[pl.ds(base, SC_LANES)]
            idx = base + lanes
            keep = (s > thr) | ((s == thr) & (idx <= j_max))
            page = plsc.load_gather(pt_v, [lax.shift_right_logical(idx, 7)])
            phys = page * PAGE_SIZE + (idx & (PAGE_SIZE - 1))
            pos = cnt + plsc.cumsum(jnp.where(keep, 1, 0)) - 1
            keep = keep & (pos < n_sel)
            plsc.store_scatter(list_v, [pos], phys, mask=keep)
            return cnt + plsc.all_reduce_population_count(keep)

        cnt = lax.fori_loop(0, past // SC_LANES, body, zero)
        cnt_v[...] = cnt
        pltpu.sync_copy(cnt_v, cnt_hbm.at[pl.ds(r * SC_LANES, SC_LANES)])

        n_pieces = n_sel // GATHER_ROWS
        jobs = [(kc_hbm, ksel_hbm, p) for p in range(n_pieces)] + [(vc_hbm, vsel_hbm, p) for p in range(n_pieces)]

        def gather(job, slot):
            src, _, p = job
            return pltpu.async_copy(src.at[list_v.at[pl.ds(p * GATHER_ROWS, GATHER_ROWS)]], bufs[slot], sems[slot])

        pending = gather(jobs[0], 0)
        for n, job in enumerate(jobs):
            pending.wait()
            if n + 1 < len(jobs):
                pending = gather(jobs[n + 1], (n + 1) % 2)
            _, dst, p = job
            pltpu.sync_copy(bufs[n % 2], dst.at[pl.ds(r * n_sel + p * GATHER_ROWS, GATHER_ROWS)])


def _sample_gather(scores_past, thr, j_max, page_table, cache_k_rows, cache_v_rows, n_sel):
    db, n_q, past = scores_past.shape
    rows = db * n_q
    n_pages = page_table.shape[1]
    info = plsc.get_sparse_core_info()
    assert info.num_lanes == SC_LANES and info.num_cores * info.num_subcores == SC_WORKERS
    assert rows % SC_WORKERS == 0 and n_sel % GATHER_ROWS == 0 and past % SC_LANES == 0
    mesh = plsc.VectorSubcoreMesh(core_axis_name="c", subcore_axis_name="s")
    sel = jax.ShapeDtypeStruct((rows * n_sel, N_HEADS, HEAD_DIM), F32)
    body = functools.partial(_sc_gather_kernel, n_q=n_q, past=past, n_pages=n_pages, n_sel=n_sel,
                             rows_per_worker=rows // SC_WORKERS, n_cores=info.num_cores)
    return pl.kernel(
        body,
        out_type=[sel, sel, jax.ShapeDtypeStruct((rows * SC_LANES,), I32)],
        mesh=mesh,
        scratch_types=[pltpu.VMEM((past,), F32), pltpu.VMEM((SC_LANES,), F32), pltpu.VMEM((SC_LANES,), I32),
                       pltpu.VMEM((n_pages,), I32), pltpu.VMEM((n_sel,), I32), pltpu.VMEM((SC_LANES,), I32),
                       pltpu.VMEM((GATHER_ROWS, N_HEADS, HEAD_DIM), F32),
                       pltpu.VMEM((GATHER_ROWS, N_HEADS, HEAD_DIM), F32),
                       pltpu.SemaphoreType.DMA, pltpu.SemaphoreType.DMA],
        compiler_params=pltpu.CompilerParams(needs_layout_passes=False),
        name="sample_gather",
    )(scores_past.reshape(-1), thr.reshape(-1), j_max.reshape(-1), page_table.reshape(-1),
      cache_k_rows, cache_v_rows)


def _sample_attn_kernel(cnt_ref, q_ref, k_ref, v_ref, sn_ref, thr_ref, jmax_ref, kn_ref, vn_ref, ga_ref, o_ref,
                        *, past, n_sel):
    r = pl.program_id(0)
    q8 = q_ref[...]

    def flat(rows_ref):
        heads = [rows_ref[pl.ds(h, n_sel, stride=N_HEADS), :] for h in range(N_HEADS)]
        return jnp.concatenate(heads, axis=1).astype(BF16)

    slot = lax.broadcasted_iota(I32, (N_HEADS, n_sel), 1)
    s_p = jnp.where(slot < cnt_ref[r], _dot_t(q8, flat(k_ref)), NEG_BIG)
    sn = sn_ref[...]
    idx_n = past + lax.broadcasted_iota(I32, sn.shape, 1)
    thr = thr_ref[:, 0:1]
    keep_n = (sn > thr) | ((sn == thr) & (idx_n <= jmax_ref[:, 0:1]))
    s_n = _dot_t(q8, kn_ref[...]) + jnp.where(keep_n, 0.0, NEG_BIG)

    m = jnp.maximum(jnp.max(s_p, axis=1, keepdims=True), jnp.max(s_n, axis=1, keepdims=True))
    p_p = jnp.exp(s_p - m)
    p_n = jnp.exp(s_n - m)
    den = jnp.sum(p_p, axis=1, keepdims=True) + jnp.sum(p_n, axis=1, keepdims=True)
    out8 = jnp.dot(p_p.astype(BF16), flat(v_ref), preferred_element_type=F32)
    out8 = (out8 + jnp.dot(p_n.astype(BF16), vn_ref[...], preferred_element_type=F32)) / den
    row = lax.broadcasted_iota(I32, out8.shape, 0)
    col_head = lax.broadcasted_iota(I32, out8.shape, 1) // HEAD_DIM
    o_ref[...] = jnp.sum(jnp.where(row == col_head, out8, 0.0), axis=0, keepdims=True) * ga_ref[...]


def _sample_attention(cnt, q8, k_sel, v_sel, scores_new, thr, j_max, k_new_pad, v_new_pad, ga, n_q, past, n_sel):
    rows = q8.shape[0]
    row3 = lambda *shape: pl.BlockSpec((None,) + shape, lambda r, c: (r, 0, 0))
    per_b = lambda *shape: pl.BlockSpec((None,) + shape, lambda r, c: (r // n_q, 0, 0))
    grid_spec = pltpu.PrefetchScalarGridSpec(
        num_scalar_prefetch=1,
        grid=(rows,),
        in_specs=[row3(N_HEADS, D_ATTN), row3(n_sel * N_HEADS, HEAD_DIM), row3(n_sel * N_HEADS, HEAD_DIM),
                  row3(1, LANES), row3(1, LANES), row3(1, LANES),
                  per_b(LANES, D_ATTN), per_b(LANES, D_ATTN), row3(1, D_ATTN)],
        out_specs=row3(1, D_ATTN),
    )
    as_rows = lambda a: a.reshape(rows, 1, a.shape[-1])
    return pl.pallas_call(
        functools.partial(_sample_attn_kernel, past=past, n_sel=n_sel),
        out_shape=jax.ShapeDtypeStruct((rows, 1, D_ATTN), F32),
        grid_spec=grid_spec,
        compiler_params=_params("parallel"),
        name="sample_attn",
    )(cnt, q8, k_sel.reshape(rows, n_sel * N_HEADS, HEAD_DIM), v_sel.reshape(rows, n_sel * N_HEADS, HEAD_DIM),
      as_rows(scores_new), as_rows(thr), as_rows(j_max), k_new_pad, v_new_pad, as_rows(ga))


CONV_TC = 256
CONV_HALO = 32
CONV_RB = 32


def _conv_finish(y, cb, g, b, gate):
    y = _layer_norm(y + cb, g, b)
    return y * _sigmoid(y) * gate


def _conv_prompt_kernel(u_ref, prev_ref, head_ref, gc_ref, w_ref, cb_ref, g_ref, b_ref, o_ref, win_ref):
    i = pl.program_id(1)

    @pl.when(i == 0)
    def _():
        win_ref[0:CONV_HALO, :] = head_ref[...]

    @pl.when(i > 0)
    def _():
        win_ref[0:CONV_HALO, :] = prev_ref[...]

    win_ref[CONV_HALO:, :] = u_ref[...]
    off = CONV_HALO - (CONV_W - 1)
    for r0 in range(0, CONV_TC, CONV_RB):
        acc = jnp.zeros((CONV_RB, u_ref.shape[1]), F32)
        for t in range(CONV_W):
            acc = acc + win_ref[r0 + off + t:r0 + off + t + CONV_RB, :] * w_ref[t:t + 1, :]
        gate = gc_ref[r0:r0 + CONV_RB, :].astype(F32)
        o_ref[r0:r0 + CONV_RB, :] = _conv_finish(acc, cb_ref[...], g_ref[...], b_ref[...], gate).astype(o_ref.dtype)


def _conv_prompt(u, head, gc, conv_w, conv_b, g, b, n_batch, seq):
    c = u.shape[-1]
    u3 = u.reshape(n_batch, seq, c)
    gc3 = gc.reshape(n_batch, seq, c)
    ratio = CONV_TC // CONV_HALO
    vec = lambda: pl.BlockSpec((1, c), lambda bb, i: (0, 0))
    out = pl.pallas_call(
        _conv_prompt_kernel,
        out_shape=jax.ShapeDtypeStruct((n_batch, seq, c), BF16),
        grid=(n_batch, seq // CONV_TC),
        in_specs=[pl.BlockSpec((None, CONV_TC, c), lambda bb, i: (bb, i, 0)),
                  pl.BlockSpec((None, CONV_HALO, c), lambda bb, i: (bb, jnp.maximum(i * ratio - 1, 0), 0)),
                  pl.BlockSpec((CONV_HALO, c), lambda bb, i: (0, 0)),
                  pl.BlockSpec((None, CONV_TC, c), lambda bb, i: (bb, i, 0)),
                  pl.BlockSpec((CONV_W, c), lambda bb, i: (0, 0)),
                  vec(), vec(), vec()],
        out_specs=pl.BlockSpec((None, CONV_TC, c), lambda bb, i: (bb, i, 0)),
        scratch_shapes=[pltpu.VMEM((CONV_HALO + CONV_TC, c), F32)],
        compiler_params=_params("parallel", "arbitrary"),
        name="conv_prompt",
    )(u3, u3, head, gc3, conv_w, conv_b.reshape(1, c), g.reshape(1, c), b.reshape(1, c))
    return out.reshape(n_batch * seq, c)


def _conv_sample_kernel(st_ref, u_ref, gc_ref, w_ref, cb_ref, g_ref, b_ref, o_ref):
    n_hist = st_ref.shape[0]
    n_new = u_ref.shape[0]
    for r in range(n_new):
        acc = jnp.zeros(u_ref.shape[1:], F32)
        for t in range(CONV_W):
            src = r + t
            row = st_ref[src] if src < n_hist else u_ref[src - n_hist]
            acc = acc + row * w_ref[t:t + 1, :]
        o_ref[r] = _conv_finish(acc, cb_ref[...], g_ref[...], b_ref[...], gc_ref[r])


def _conv_sample(state_t, u_t, gc_t, conv_w, conv_b, g, b):
    c = u_t.shape[-1]
    return pl.pallas_call(
        _conv_sample_kernel,
        out_shape=jax.ShapeDtypeStruct(u_t.shape, F32),
        compiler_params=pltpu.CompilerParams(vmem_limit_bytes=VMEM_LIMIT),
        name="conv_sample",
    )(state_t, u_t, gc_t, conv_w, conv_b.reshape(1, c), g.reshape(1, c), b.reshape(1, c))


def _out_kernel(x_ref, a_ref, c_ref, wa_ref, wc_ref, gi_ref, bi_ref, go_ref, bo_ref, o_ref, *, alpha):
    h = _layer_norm(x_ref[...], gi_ref[...], bi_ref[...])
    z = jnp.dot(a_ref[...], wa_ref[...], preferred_element_type=F32)
    z = z + jnp.dot(c_ref[...], wc_ref[...], preferred_element_type=F32)
    o_ref[...] = _layer_norm(alpha * h + z, go_ref[...], bo_ref[...])


def _out_projection(x, a, c, wa, wc, gi, bi, go, bo, alpha, tm):
    m, d = x.shape
    row = lambda w: pl.BlockSpec((tm, w), lambda i: (i, 0))
    full = lambda arr: pl.BlockSpec(arr.shape, lambda i: (0, 0))
    vecs = [v.reshape(1, d) for v in (gi, bi, go, bo)]
    return pl.pallas_call(
        functools.partial(_out_kernel, alpha=alpha),
        out_shape=jax.ShapeDtypeStruct((m, d), F32),
        grid=(m // tm,),
        in_specs=[row(d), row(a.shape[1]), row(c.shape[1]), full(wa), full(wc)] + [full(v) for v in vecs],
        out_specs=row(d),
        compiler_params=_params("parallel"),
        name="out_proj",
    )(x, a, c, wa, wc, *vecs)


def _split_w_in(w):
    d_conv = (w.shape[1] - (3 * D_ATTN + N_IDX_HEADS * D_IDX + D_IDX + N_IDX_HEADS + D_ATTN)) // 3
    o = [0]
    for n in (D_ATTN, 2 * D_ATTN, N_IDX_HEADS * D_IDX, D_IDX + N_IDX_HEADS, D_ATTN, d_conv, d_conv, d_conv):
        o.append(o[-1] + n)
    piece = lambda i: w[:, o[i]:o[i + 1]].astype(BF16)
    kiwi = jnp.pad(w[:, o[3]:o[4]], ((0, 0), (0, 2 * LANES - (D_IDX + N_IDX_HEADS)))).astype(BF16)
    return dict(q=piece(0), kv=piece(1), qi=piece(2), kiwi=kiwi, ga=piece(4),
                glu_a=piece(5), glu_b=piece(6), gc=piece(7))


def kernel(x_prompt, x_sample, cache_k, cache_v, cache_kidx, state_conv, page_table, meta_tokens,
           ln_in_g, ln_in_b, w_in, ln_kidx_g, ln_kidx_b, conv_w, conv_b, ln_conv_g, ln_conv_b,
           w_out, ln_out_g, ln_out_b):
    n_batch, seq, d_model = x_prompt.shape
    db, n_new, _ = x_sample.shape
    depth = w_in.shape[0]
    assert depth == 1, "one mixer layer per step"
    assert seq >= CONV_W - 1
    n_pages = page_table.shape[1]
    past = n_pages * PAGE_SIZE
    n_sel_prompt = min(TOPK_MAX, seq // 4)
    n_sel_sample = min(TOPK_MAX, (past + n_new) // 4)
    alpha = (2.0 * depth) ** 0.25
    l = 0

    w = _split_w_in(w_in[l])
    d_conv = w["gc"].shape[1]
    wo = w_out[l].astype(BF16)
    wo_a, wo_c = wo[:D_ATTN], wo[D_ATTN:]

    xp = x_prompt.reshape(n_batch * seq, d_model)
    xs = jnp.concatenate([meta_tokens.astype(F32), x_sample.reshape(db * n_new, d_model)], axis=0)
    n_small = xs.shape[0]
    hp = _ln_cast(xp, ln_in_g, ln_in_b, 512)
    hs = _ln_cast(xs, ln_in_g, ln_in_b, n_small)
    pp = _in_projection(hp, w, ln_kidx_g[l], ln_kidx_b[l], 512)
    ps = _in_projection(hs, w, ln_kidx_g[l], ln_kidx_b[l], n_small)
    pm = {n: a[:N_META] for n, a in ps.items()}
    ps = {n: a[N_META:] for n, a in ps.items()}

    pad_rows = lambda a: jnp.pad(a, ((0, LANES - a.shape[0]), (0, 0)))
    meta = dict(ki=pad_rows(pm["ki16"]), k=pad_rows(pm["kv16"][:, :D_ATTN]),
                vt=jnp.transpose(pad_rows(pm["kv16"][:, D_ATTN:])))
    a_p = _prompt_attention(pp, meta, n_batch, seq, n_sel_prompt).reshape(n_batch * seq, D_ATTN)
    head = jnp.concatenate([jnp.zeros((CONV_HALO - N_META, d_conv), F32), pm["u"]], axis=0)
    c_p = _conv_prompt(pp["u"], head, pp["gc"], conv_w[l], conv_b[l], ln_conv_g[l], ln_conv_b[l], n_batch, seq)
    y_p = _out_projection(xp, a_p, c_p, wo_a, wo_c, ln_in_g, ln_in_b, ln_out_g[l], ln_out_b[l], alpha, 256)
    y_prompt = y_p.reshape(n_batch, seq, d_model)

    def with_meta(m_rows, p_rows):
        m_b = jnp.broadcast_to(m_rows[None], (n_batch,) + m_rows.shape)
        return jnp.concatenate([m_b, p_rows.reshape(n_batch, seq, -1)], axis=1)

    new_k_p = with_meta(pm["kv32"][:, :D_ATTN], pp["kv32"][:, :D_ATTN]).reshape(1, n_batch, N_META + seq, N_HEADS, HEAD_DIM)
    new_v_p = with_meta(pm["kv32"][:, D_ATTN:], pp["kv32"][:, D_ATTN:]).reshape(1, n_batch, N_META + seq, N_HEADS, HEAD_DIM)
    new_ki_p = with_meta(pm["ki32"], pp["ki32"])[None]
    new_conv_p = pp["u"].reshape(n_batch, seq, d_conv)[:, -(CONV_W - 1):][None]

    qi_rows = ps["qi"].reshape(db, n_new * N_IDX_HEADS, D_IDX)
    wcol = jnp.broadcast_to(ps["wi"][:, :N_IDX_HEADS].reshape(db, n_new * N_IDX_HEADS, 1),
                            (db, n_new * N_IDX_HEADS, LANES))
    pad_new = lambda a: jnp.pad(a.reshape(db, n_new, a.shape[-1]), ((0, 0), (0, LANES - n_new), (0, 0)))
    sc_past, sc_new = _sample_scores(page_table, qi_rows, wcol, pad_new(ps["ki16"]), cache_kidx, l)
    thr, j_max = _sample_threshold(sc_past, sc_new, n_sel_sample)
    key_rows = lambda c: c.reshape(-1, N_HEADS, HEAD_DIM)
    n_pool = cache_k.shape[1]
    k_sel, v_sel, cnt = _sample_gather(sc_past, thr, j_max, page_table + l * n_pool,
                                       key_rows(cache_k), key_rows(cache_v), n_sel_sample)
    q4 = ps["q"].reshape(db * n_new, N_HEADS, HEAD_DIM)
    q8 = jnp.einsum("rhd,hg->rhgd", q4, jnp.eye(N_HEADS, dtype=BF16)).reshape(db * n_new, N_HEADS, D_ATTN)
    a_s = _sample_attention(cnt.reshape(db * n_new, SC_LANES)[:, 0], q8, k_sel, v_sel, sc_new, thr, j_max,
                            pad_new(ps["kv16"][:, :D_ATTN]), pad_new(ps["kv16"][:, D_ATTN:]),
                            ps["ga"].astype(F32), n_new, past, n_sel_sample)
    a_s = a_s.reshape(db * n_new, D_ATTN).astype(BF16)

    to_t = lambda a: jnp.transpose(a.reshape(db, -1, d_conv), (1, 0, 2))
    c_t = _conv_sample(to_t(state_conv[l].astype(F32)), to_t(ps["u"]), to_t(ps["gc"].astype(F32)),
                       conv_w[l], conv_b[l], ln_conv_g[l], ln_conv_b[l])
    c_s = jnp.transpose(c_t, (1, 0, 2)).reshape(db * n_new, d_conv).astype(BF16)
    y_s = _out_projection(x_sample.reshape(db * n_new, d_model), a_s, c_s, wo_a, wo_c,
                          ln_in_g, ln_in_b, ln_out_g[l], ln_out_b[l], alpha, db * n_new)
    y_sample = y_s.reshape(db, n_new, d_model)

    new_k_s = ps["kv32"][:, :D_ATTN].reshape(1, db, n_new, N_HEADS, HEAD_DIM)
    new_v_s = ps["kv32"][:, D_ATTN:].reshape(1, db, n_new, N_HEADS, HEAD_DIM)
    new_ki_s = ps["ki32"].reshape(1, db, n_new, D_IDX)
    u_ext_s = jnp.concatenate([state_conv[l].astype(F32), ps["u"].reshape(db, n_new, d_conv)], axis=1)
    new_conv_s = u_ext_s[:, -(CONV_W - 1):][None]

    return (y_prompt, y_sample, new_k_p, new_v_p, new_ki_p, new_conv_p,
            new_k_s, new_v_s, new_ki_s, new_conv_s)
```

```python
import functools

import jax
import jax.numpy as jnp
from jax import lax
from jax.experimental import pallas as pl
from jax.experimental.pallas import tpu as pltpu
from jax.experimental.pallas import tpu_sc as plsc

N_META = 16
N_HEADS = 8
HEAD_DIM = 128
D_ATTN = N_HEADS * HEAD_DIM
N_IDX_HEADS = 16
D_IDX = 128
TOPK_MAX = 256
CONV_W = 31
PAGE_SIZE = 128
LN_EPS = 1e-5
ATTN_SCALE = HEAD_DIM ** -0.5
IDX_SCALE = (N_IDX_HEADS * D_IDX) ** -0.5

LANES = 128
SUBLANES = 8
NEG_BIG = -1e30
F32_MAX = 3.4028235e38
INT_MAX = 2 ** 31 - 1
SEARCH_CAP = 400
VMEM_LIMIT = 56 * 1024 * 1024

F32 = jnp.float32
BF16 = jnp.bfloat16
I32 = jnp.int32


def _params(*sem):
    return pltpu.CompilerParams(dimension_semantics=sem, vmem_limit_bytes=VMEM_LIMIT)


def _layer_norm(x, g, b):
    mu = jnp.mean(x, axis=-1, keepdims=True)
    xc = x - mu
    var = jnp.mean(xc * xc, axis=-1, keepdims=True)
    return xc * lax.rsqrt(var + LN_EPS) * g + b


def _sigmoid(z):
    return 1.0 / (1.0 + jnp.exp(-z))


def _dot_t(a, b):
    return lax.dot_general(a, b, (((1,), (1,)), ((), ())), preferred_element_type=F32)


def _fold_rows(x, reduce):
    return reduce(x.reshape(x.shape[0] // SUBLANES, SUBLANES, x.shape[1]), axis=0)


def _col_reduce(x, reduce):
    return reduce(_fold_rows(x, reduce), axis=0, keepdims=True)


def _topk_threshold(count, lo0, hi0, c0, active, k, n_total):
    one = jnp.ones(lo0.shape, I32)

    def cond(st):
        it, _, _, c_lo, stalled = st
        waiting = jnp.sum(jnp.where(active & (c_lo != k) & (stalled == 0), one, 0))
        return (it < SEARCH_CAP) & (waiting > 0)

    def step(st):
        it, lo, hi, c_lo, stalled = st
        mid = 0.5 * lo + 0.5 * hi
        stall = (mid <= lo) | (mid >= hi)
        c = count(lambda s, i: s >= mid)
        up = (c >= k) & jnp.logical_not(stall)
        down = (c < k) & jnp.logical_not(stall)
        return (it + 1, jnp.where(up, mid, lo), jnp.where(down, mid, hi), jnp.where(up, c, c_lo),
                jnp.where(stall, one, stalled))

    def body(st):
        return step(step(st))

    _, thr, _, c_lo, _ = lax.while_loop(cond, body, (jnp.int32(0), lo0, hi0, c0, jnp.zeros(lo0.shape, I32)))
    tied = active & (c_lo > k)

    def break_ties(_):
        need = k - count(lambda s, i: s > thr)

        def step(_, st):
            j_lo, j_hi = st
            j_mid = j_lo + (j_hi - j_lo) // 2
            ok = count(lambda s, i: (s == thr) & (i <= j_mid)) >= need
            return jnp.where(ok, j_lo, j_mid), jnp.where(ok, j_mid, j_hi)

        n_steps = (n_total + 1).bit_length()
        _, j_hi = lax.fori_loop(0, n_steps, step, (jnp.full(lo0.shape, -1, I32), jnp.full(lo0.shape, n_total, I32)))
        return jnp.where(tied, j_hi, INT_MAX)

    j_max = lax.cond(jnp.sum(jnp.where(tied, one, 0)) > 0, break_ties,
                     lambda _: jnp.full(lo0.shape, INT_MAX, I32), 0)
    thr = jnp.where(active, thr, -F32_MAX)
    return thr, j_max


def _above(x):
    return x + (jnp.abs(x) * 2.0 ** -20 + 1e-30)


def _ln_kernel(x_ref, g_ref, b_ref, o_ref):
    o_ref[...] = _layer_norm(x_ref[...], g_ref[...], b_ref[...]).astype(o_ref.dtype)


def _ln_cast(x, g, b, tm):
    m, d = x.shape
    return pl.pallas_call(
        _ln_kernel,
        out_shape=jax.ShapeDtypeStruct((m, d), BF16),
        grid=(m // tm,),
        in_specs=[pl.BlockSpec((tm, d), lambda i: (i, 0)),
                  pl.BlockSpec((1, d), lambda i: (0, 0)),
                  pl.BlockSpec((1, d), lambda i: (0, 0))],
        out_specs=pl.BlockSpec((tm, d), lambda i: (i, 0)),
        compiler_params=_params("parallel"),
        name="ln_in",
    )(x, g.reshape(1, d), b.reshape(1, d))


def _proj_q_kernel(h_ref, w_ref, o_ref):
    z = jnp.dot(h_ref[...], w_ref[...], preferred_element_type=F32)
    o_ref[...] = (z * ATTN_SCALE).astype(o_ref.dtype)


def _proj_cast_kernel(h_ref, w_ref, o_ref):
    o_ref[...] = jnp.dot(h_ref[...], w_ref[...], preferred_element_type=F32).astype(o_ref.dtype)


def _proj_kv_kernel(h_ref, w_ref, k32_ref, v32_ref, k16_ref, v16_ref):
    z = jnp.dot(h_ref[...], w_ref[...], preferred_element_type=F32)
    k, v = z[:, :D_ATTN], z[:, D_ATTN:]
    k32_ref[...] = k
    v32_ref[...] = v
    k16_ref[...] = k.astype(k16_ref.dtype)
    v16_ref[...] = v.astype(v16_ref.dtype)


def _proj_silu_kernel(h_ref, w_ref, o_ref):
    z = jnp.dot(h_ref[...], w_ref[...], preferred_element_type=F32)
    o_ref[...] = (z * _sigmoid(z)).astype(o_ref.dtype)


def _proj_glu_kernel(h_ref, wa_ref, wb_ref, o_ref):
    h = h_ref[...]
    za = jnp.dot(h, wa_ref[...], preferred_element_type=F32)
    zb = jnp.dot(h, wb_ref[...], preferred_element_type=F32)
    o_ref[...] = za * _sigmoid(zb)


def _proj_kiwi_kernel(h_ref, w_ref, g_ref, b_ref, ki32_ref, ki16_ref, wi_ref):
    z = jnp.dot(h_ref[...], w_ref[...], preferred_element_type=F32)
    ki = _layer_norm(z[:, :D_IDX], g_ref[...], b_ref[...])
    ki32_ref[...] = ki
    ki16_ref[...] = ki.astype(ki16_ref.dtype)
    wi_ref[...] = z[:, D_IDX:] * IDX_SCALE


def _proj(kernel, h, ws, extra, outs, tm, name):
    m, d = h.shape
    in_specs = [pl.BlockSpec((tm, d), lambda i: (i, 0))]
    in_specs += [pl.BlockSpec(w.shape, lambda i: (0, 0)) for w in ws]
    in_specs += [pl.BlockSpec(e.shape, lambda i: (0, 0)) for e in extra]
    out_shape = [jax.ShapeDtypeStruct((m, n), dt) for n, dt in outs]
    out_specs = [pl.BlockSpec((tm, n), lambda i: (i, 0)) for n, _ in outs]
    return pl.pallas_call(
        kernel,
        out_shape=out_shape,
        grid=(m // tm,),
        in_specs=in_specs,
        out_specs=out_specs,
        compiler_params=_params("parallel"),
        name=name,
    )(h, *ws, *extra)


def _in_projection(h, w, kidx_g, kidx_b, tm):
    (q,) = _proj(_proj_q_kernel, h, [w["q"]], [], [(D_ATTN, BF16)], tm, "proj_q")
    k32, v32, k16, v16 = _proj(_proj_kv_kernel, h, [w["kv"]], [],
                               [(D_ATTN, F32), (D_ATTN, F32), (D_ATTN, BF16), (D_ATTN, BF16)], tm, "proj_kv")
    (qi,) = _proj(_proj_cast_kernel, h, [w["qi"]], [], [(N_IDX_HEADS * D_IDX, BF16)], tm, "proj_qi")
    ki32, ki16, wi = _proj(_proj_kiwi_kernel, h, [w["kiwi"]], [kidx_g.reshape(1, D_IDX), kidx_b.reshape(1, D_IDX)],
                           [(D_IDX, F32), (D_IDX, BF16), (LANES, F32)], tm, "proj_kiwi")
    (ga,) = _proj(_proj_silu_kernel, h, [w["ga"]], [], [(D_ATTN, BF16)], tm, "proj_gattn")
    (u,) = _proj(_proj_glu_kernel, h, [w["glu_a"], w["glu_b"]], [], [(w["glu_a"].shape[1], F32)], tm, "proj_glu")
    (gc,) = _proj(_proj_silu_kernel, h, [w["gc"]], [], [(w["gc"].shape[1], BF16)], tm, "proj_gconv")
    return dict(q=q, k32=k32, v32=v32, k16=k16, v16=v16, qi=qi, ki32=ki32, ki16=ki16, wi=wi, ga=ga, u=u, gc=gc)


TQ = 256
KC = 256


def _prompt_attn_kernel(qi_ref, wi_ref, q_ref, ga_ref, kip_ref, kp_ref, vtp_ref, kim_ref, km_ref, vtm_ref,
                        o_ref, sc_ref, scm_ref, wt_ref, s_ref, m_ref, l_ref, acc_ref, *, n_sel):
    qb = pl.program_id(1)
    n_chunks = (qb * TQ + TQ - 1) // KC + 1
    q_pos = qb * TQ + lax.broadcasted_iota(I32, (1, TQ), 1)
    row_kc = lax.broadcasted_iota(I32, (KC, TQ), 0)
    row_m = lax.broadcasted_iota(I32, (LANES, TQ), 0)

    wt_ref[...] = jnp.transpose(wi_ref[...])

    def idx_scores(ki):
        acc = jnp.zeros((ki.shape[0], TQ), F32)
        for h in range(N_IDX_HEADS):
            s = _dot_t(ki, qi_ref[:, h * D_IDX:(h + 1) * D_IDX])
            acc = acc + jnp.maximum(s, 0.0) * wt_ref[h:h + 1, :]
        return acc

    sm = idx_scores(kim_ref[...])
    meta_vis = row_m < N_META
    scm_ref[...] = jnp.where(meta_vis, sm, -jnp.inf)

    def score_chunk(j, carry):
        rmax, rmin = carry
        r0 = pl.multiple_of(j * KC, KC)
        s = idx_scores(kip_ref[pl.ds(r0, KC), :])
        vis = (j * KC + row_kc) <= q_pos
        s_vis = jnp.where(vis, s, -jnp.inf)
        sc_ref[j] = s_vis
        rmax = jnp.maximum(rmax, _fold_rows(s_vis, jnp.max))
        rmin = jnp.minimum(rmin, _fold_rows(jnp.where(vis, s, jnp.inf), jnp.min))
        return rmax, rmin

    rmax, rmin = lax.fori_loop(0, n_chunks, score_chunk,
                               (_fold_rows(jnp.where(meta_vis, sm, -jnp.inf), jnp.max),
                                _fold_rows(jnp.where(meta_vis, sm, jnp.inf), jnp.min)))

    def count(f):
        def ones(mask):
            return _fold_rows(jnp.where(mask, 1, 0), jnp.sum)

        def body(j, c):
            return c + ones(f(sc_ref[j], LANES + j * KC + row_kc))

        c = lax.fori_loop(0, n_chunks, body, ones(f(scm_ref[...], row_m)))
        return jnp.sum(c, axis=0, keepdims=True)

    n_vis = N_META + 1 + q_pos
    thr, j_max = _topk_threshold(count, jnp.min(rmin, axis=0, keepdims=True),
                                 _above(jnp.max(rmax, axis=0, keepdims=True)), n_vis,
                                 n_vis > n_sel, n_sel, LANES + sc_ref.shape[0] * KC)

    def bias_of(s, idx):
        keep = (s > thr) | ((s == thr) & (idx <= j_max))
        return jnp.where(keep, 0.0, NEG_BIG)

    m_ref[...] = jnp.full(m_ref.shape, NEG_BIG, F32)
    l_ref[...] = jnp.zeros(l_ref.shape, F32)
    acc_ref[...] = jnp.zeros(acc_ref.shape, F32)

    def attend(bias, k_blk, vt_blk):
        n = bias.shape[0]
        heads = [slice(h * HEAD_DIM, (h + 1) * HEAD_DIM) for h in range(N_HEADS)]
        m_new = []
        for h, cs in enumerate(heads):
            s = _dot_t(k_blk[:, cs], q_ref[:, cs]) + bias
            s_ref[h, 0:n, :] = s
            m_new.append(jnp.maximum(m_ref[h], _col_reduce(s, jnp.max)))
        for h, cs in enumerate(heads):
            alpha = jnp.exp(m_ref[h] - m_new[h])
            p = jnp.exp(s_ref[h, 0:n, :] - m_new[h])
            l_ref[h] = alpha * l_ref[h] + _col_reduce(p, jnp.sum)
            pv = jnp.dot(vt_blk[cs, :], p.astype(BF16), preferred_element_type=F32)
            acc_ref[cs, :] = alpha * acc_ref[cs, :] + pv
            m_ref[h] = m_new[h]

    attend(bias_of(scm_ref[...], row_m), km_ref[...], vtm_ref[...])

    def attn_chunk(j, carry):
        r0 = pl.multiple_of(j * KC, KC)
        attend(bias_of(sc_ref[j], LANES + j * KC + row_kc), kp_ref[pl.ds(r0, KC), :], vtp_ref[j])
        return carry

    lax.fori_loop(0, n_chunks, attn_chunk, 0)

    for h in range(N_HEADS):
        cs = slice(h * HEAD_DIM, (h + 1) * HEAD_DIM)
        out = jnp.transpose(acc_ref[cs, :] / l_ref[h])
        o_ref[:, cs] = (out * ga_ref[:, cs].astype(F32)).astype(o_ref.dtype)


def _prompt_attention(p, meta, n_batch, seq, n_sel):
    def rows(a):
        return a.reshape(n_batch, seq, a.shape[-1])

    qi, wi, q, ga, ki16, k16 = (rows(p[n]) for n in ("qi", "wi", "q", "ga", "ki16", "k16"))
    n_kc = seq // KC
    vt = jnp.transpose(p["v16"].reshape(n_batch, n_kc, KC, D_ATTN), (0, 1, 3, 2))
    blk = lambda w: pl.BlockSpec((None, TQ, w), lambda b, i: (b, i, 0))
    res = lambda w: pl.BlockSpec((None, seq, w), lambda b, i: (b, 0, 0), pipeline_mode=pl.Buffered(1))
    const = lambda a: pl.BlockSpec(a.shape, lambda b, i: (0, 0), pipeline_mode=pl.Buffered(1))
    return pl.pallas_call(
        functools.partial(_prompt_attn_kernel, n_sel=n_sel),
        out_shape=jax.ShapeDtypeStruct((n_batch, seq, D_ATTN), BF16),
        grid=(n_batch, seq // TQ),
        in_specs=[blk(N_IDX_HEADS * D_IDX), blk(LANES), blk(D_ATTN), blk(D_ATTN),
                  res(D_IDX), res(D_ATTN),
                  pl.BlockSpec((None, n_kc, D_ATTN, KC), lambda b, i: (b, 0, 0, 0), pipeline_mode=pl.Buffered(1)),
                  const(meta["ki"]), const(meta["k"]), const(meta["vt"])],
        out_specs=blk(D_ATTN),
        scratch_shapes=[pltpu.VMEM((n_kc, KC, TQ), F32),
                        pltpu.VMEM((LANES, TQ), F32),
                        pltpu.VMEM((LANES, TQ), F32),
                        pltpu.VMEM((N_HEADS, KC, TQ), F32),
                        pltpu.VMEM((N_HEADS, 1, TQ), F32),
                        pltpu.VMEM((N_HEADS, 1, TQ), F32),
                        pltpu.VMEM((D_ATTN, TQ), F32)],
        compiler_params=_params("parallel", "arbitrary"),
        name="prompt_attn",
    )(qi, wi, q, ga, ki16, k16, vt, meta["ki"], meta["k"], meta["vt"])


SCORE_PAGES = 32


def _idx_rows(qi, wcol, ki, n_q):
    a = jnp.maximum(_dot_t(qi, ki), 0.0) * wcol
    return jnp.sum(a.reshape(n_q, N_IDX_HEADS, ki.shape[0]), axis=1)


def _sample_score_kernel(pt_ref, qi_ref, wcol_ref, kin_ref, *refs):
    pages, (o_ref, on_ref) = refs[:SCORE_PAGES], refs[SCORE_PAGES:]
    n_q = o_ref.shape[0]
    qi = qi_ref[...]
    wcol = wcol_ref[...]
    for r, page in enumerate(pages):
        o_ref[:, r * PAGE_SIZE:(r + 1) * PAGE_SIZE] = _idx_rows(qi, wcol, page[...].astype(BF16), n_q)

    @pl.when(pl.program_id(1) == pl.num_programs(1) - 1)
    def _():
        sc = _idx_rows(qi, wcol, kin_ref[...], n_q)
        lane = lax.broadcasted_iota(I32, (n_q, LANES), 1)
        row = lax.broadcasted_iota(I32, (n_q, LANES), 0)
        on_ref[...] = jnp.where(lane <= row, sc, -jnp.inf)


def _sample_scores(page_table, qi_rows, wcol, ki_new_pad, cache_kidx, layer):
    db, n_pages = page_table.shape
    n_rows = qi_rows.shape[1]
    n_q = n_rows // N_IDX_HEADS
    steps = n_pages // SCORE_PAGES
    per_b = lambda *shape: pl.BlockSpec((None,) + shape, lambda b, j, pt: (b, 0, 0))
    page_spec = lambda r: pl.BlockSpec((None, None, PAGE_SIZE, D_IDX),
                                       lambda b, j, pt, r=r: (layer, pt[b, j * SCORE_PAGES + r], 0, 0))
    grid_spec = pltpu.PrefetchScalarGridSpec(
        num_scalar_prefetch=1,
        grid=(db, steps),
        in_specs=[per_b(n_rows, D_IDX), per_b(n_rows, LANES), per_b(LANES, D_IDX)]
                 + [page_spec(r) for r in range(SCORE_PAGES)],
        out_specs=[pl.BlockSpec((None, n_q, SCORE_PAGES * PAGE_SIZE), lambda b, j, pt: (b, 0, j)),
                   per_b(n_q, LANES)],
    )
    return pl.pallas_call(
        _sample_score_kernel,
        out_shape=[jax.ShapeDtypeStruct((db, n_q, n_pages * PAGE_SIZE), F32),
                   jax.ShapeDtypeStruct((db, n_q, LANES), F32)],
        grid_spec=grid_spec,
        compiler_params=_params("parallel", "arbitrary"),
        name="sample_scores",
    )(page_table, qi_rows, wcol, ki_new_pad, *([cache_kidx] * SCORE_PAGES))


def _sample_thr_kernel(sp_ref, sn_ref, thr_ref, j_ref, *, n_sel, n_q):
    rows, past = sp_ref.shape
    lane = lax.broadcasted_iota(I32, (rows, LANES), 1)
    idx_p = lax.broadcasted_iota(I32, (rows, past), 1)
    sn = sn_ref[...]

    def count(f):
        c = jnp.sum(jnp.where(f(sp_ref[...], idx_p), 1, 0), axis=1, keepdims=True)
        return c + jnp.sum(jnp.where(f(sn, past + lane), 1, 0), axis=1, keepdims=True)

    new_vis = sn > -jnp.inf
    rmax = jnp.maximum(jnp.max(sp_ref[...], axis=1, keepdims=True), jnp.max(sn, axis=1, keepdims=True))
    rmin = jnp.minimum(jnp.min(sp_ref[...], axis=1, keepdims=True),
                       jnp.min(jnp.where(new_vis, sn, jnp.inf), axis=1, keepdims=True))
    n_vis = past + 1 + lax.rem(lax.broadcasted_iota(I32, (rows, 1), 0), n_q)
    thr, j_max = _topk_threshold(count, rmin, _above(rmax), n_vis, n_vis > n_sel, n_sel, past + LANES)
    thr_ref[...] = jnp.broadcast_to(thr, thr_ref.shape)
    j_ref[...] = jnp.broadcast_to(j_max, j_ref.shape)


def _sample_threshold(scores_past, scores_new, n_sel):
    db, n_q, past = scores_past.shape
    rows = db * n_q
    out = jax.ShapeDtypeStruct((rows, LANES), F32), jax.ShapeDtypeStruct((rows, LANES), I32)
    thr, j_max = pl.pallas_call(
        functools.partial(_sample_thr_kernel, n_sel=n_sel, n_q=n_q),
        out_shape=out,
        compiler_params=pltpu.CompilerParams(vmem_limit_bytes=VMEM_LIMIT),
        name="sample_threshold",
    )(scores_past.reshape(rows, past), scores_new.reshape(rows, LANES))
    return thr.reshape(db, n_q, LANES), j_max.reshape(db, n_q, LANES)


SC_LANES = 16
SC_WORKERS = 32
GATHER_ROWS = 32


def _sc_gather_kernel(sc_hbm, thr_hbm, jmax_hbm, pt_hbm, kc_hbm, vc_hbm, ksel_hbm, vsel_hbm, cnt_hbm,
                      sc_v, thr_v, jmax_v, pt_v, list_v, cnt_v, buf0, buf1, sem0, sem1,
                      *, n_q, past, n_pages, n_sel, rows_per_worker, n_cores):
    wid = lax.axis_index("s") * n_cores + lax.axis_index("c")
    lanes = lax.iota(I32, SC_LANES)
    zero = jnp.zeros((SC_LANES,), I32)
    bufs, sems = (buf0, buf1), (sem0, sem1)

    @pl.loop(0, rows_per_worker)
    def _(t):
        r = wid * rows_per_worker + t
        pltpu.sync_copy(pt_hbm.at[pl.ds((r // n_q) * n_pages, n_pages)], pt_v)
        pltpu.sync_copy(sc_hbm.at[pl.ds(r * past, past)], sc_v)
        pltpu.sync_copy(thr_hbm.at[pl.ds(r * LANES, SC_LANES)], thr_v)
        pltpu.sync_copy(jmax_hbm.at[pl.ds(r * LANES, SC_LANES)], jmax_v)
        thr = thr_v[...]
        j_max = jmax_v[...]
        for i in range(n_sel // SC_LANES):
            list_v[pl.ds(i * SC_LANES, SC_LANES)] = zero

        def body(i, cnt):
            base = i * SC_LANES
            s = sc_v[pl.ds(base, SC_LANES)]
            idx = base + lanes
            keep = (s > thr) | ((s == thr) & (idx <= j_max))
            page = plsc.load_gather(pt_v, [lax.shift_right_logical(idx, 7)])
            phys = page * PAGE_SIZE + (idx & (PAGE_SIZE - 1))
            pos = cnt + plsc.cumsum(jnp.where(keep, 1, 0)) - 1
            keep = keep & (pos < n_sel)
            plsc.store_scatter(list_v, [pos], phys, mask=keep)
            return cnt + plsc.all_reduce_population_count(keep)

        cnt = lax.fori_loop(0, past // SC_LANES, body, zero)
        cnt_v[...] = cnt
        pltpu.sync_copy(cnt_v, cnt_hbm.at[pl.ds(r * SC_LANES, SC_LANES)])

        n_pieces = n_sel // GATHER_ROWS
        jobs = [(kc_hbm, ksel_hbm, p) for p in range(n_pieces)] + [(vc_hbm, vsel_hbm, p) for p in range(n_pieces)]

        def gather(job, slot):
            src, _, p = job
            return pltpu.async_copy(src.at[list_v.at[pl.ds(p * GATHER_ROWS, GATHER_ROWS)]], bufs[slot], sems[slot])

        pending = gather(jobs[0], 0)
        for n, job in enumerate(jobs):
            pending.wait()
            if n + 1 < len(jobs):
                pending = gather(jobs[n + 1], (n + 1) % 2)
            _, dst, p = job
            pltpu.sync_copy(bufs[n % 2], dst.at[pl.ds(r * n_sel + p * GATHER_ROWS, GATHER_ROWS)])


def _sample_gather(scores_past, thr, j_max, page_table, cache_k_rows, cache_v_rows, n_sel):
    db, n_q, past = scores_past.shape
    rows = db * n_q
    n_pages = page_table.shape[1]
    info = plsc.get_sparse_core_info()
    assert info.num_lanes == SC_LANES and info.num_cores * info.num_subcores == SC_WORKERS
    assert rows % SC_WORKERS == 0 and n_sel % GATHER_ROWS == 0 and past % SC_LANES == 0
    mesh = plsc.VectorSubcoreMesh(core_axis_name="c", subcore_axis_name="s")
    sel = jax.ShapeDtypeStruct((rows * n_sel, N_HEADS, HEAD_DIM), F32)
    body = functools.partial(_sc_gather_kernel, n_q=n_q, past=past, n_pages=n_pages, n_sel=n_sel,
                             rows_per_worker=rows // SC_WORKERS, n_cores=info.num_cores)
    return pl.kernel(
        body,
        out_type=[sel, sel, jax.ShapeDtypeStruct((rows * SC_LANES,), I32)],
        mesh=mesh,
        scratch_types=[pltpu.VMEM((past,), F32), pltpu.VMEM((SC_LANES,), F32), pltpu.VMEM((SC_LANES,), I32),
                       pltpu.VMEM((n_pages,), I32), pltpu.VMEM((n_sel,), I32), pltpu.VMEM((SC_LANES,), I32),
                       pltpu.VMEM((GATHER_ROWS, N_HEADS, HEAD_DIM), F32),
                       pltpu.VMEM((GATHER_ROWS, N_HEADS, HEAD_DIM), F32),
                       pltpu.SemaphoreType.DMA, pltpu.SemaphoreType.DMA],
        compiler_params=pltpu.CompilerParams(needs_layout_passes=False),
        name="sample_gather",
    )(scores_past.reshape(-1), thr.reshape(-1), j_max.reshape(-1), page_table.reshape(-1),
      cache_k_rows, cache_v_rows)


def _sample_attn_kernel(cnt_ref, q_ref, k_ref, v_ref, sn_ref, thr_ref, jmax_ref, kn_ref, vn_ref, ga_ref, o_ref,
                        *, past, n_sel):
    r = pl.program_id(0)
    q8 = q_ref[...]

    def flat(rows_ref):
        heads = [rows_ref[pl.ds(h, n_sel, stride=N_HEADS), :] for h in range(N_HEADS)]
        return jnp.concatenate(heads, axis=1).astype(BF16)

    slot = lax.broadcasted_iota(I32, (N_HEADS, n_sel), 1)
    s_p = jnp.where(slot < cnt_ref[r], _dot_t(q8, flat(k_ref)), NEG_BIG)
    sn = sn_ref[...]
    idx_n = past + lax.broadcasted_iota(I32, sn.shape, 1)
    thr = thr_ref[:, 0:1]
    keep_n = (sn > thr) | ((sn == thr) & (idx_n <= jmax_ref[:, 0:1]))
    s_n = _dot_t(q8, kn_ref[...]) + jnp.where(keep_n, 0.0, NEG_BIG)

    m = jnp.maximum(jnp.max(s_p, axis=1, keepdims=True), jnp.max(s_n, axis=1, keepdims=True))
    p_p = jnp.exp(s_p - m)
    p_n = jnp.exp(s_n - m)
    den = jnp.sum(p_p, axis=1, keepdims=True) + jnp.sum(p_n, axis=1, keepdims=True)
    out8 = jnp.dot(p_p.astype(BF16), flat(v_ref), preferred_element_type=F32)
    out8 = (out8 + jnp.dot(p_n.astype(BF16), vn_ref[...], preferred_element_type=F32)) / den
    row = lax.broadcasted_iota(I32, out8.shape, 0)
    col_head = lax.broadcasted_iota(I32, out8.shape, 1) // HEAD_DIM
    o_ref[...] = jnp.sum(jnp.where(row == col_head, out8, 0.0), axis=0, keepdims=True) * ga_ref[...]


def _sample_attention(cnt, q8, k_sel, v_sel, scores_new, thr, j_max, k_new_pad, v_new_pad, ga, n_q, past, n_sel):
    rows = q8.shape[0]
    row3 = lambda *shape: pl.BlockSpec((None,) + shape, lambda r, c: (r, 0, 0))
    per_b = lambda *shape: pl.BlockSpec((None,) + shape, lambda r, c: (r // n_q, 0, 0))
    grid_spec = pltpu.PrefetchScalarGridSpec(
        num_scalar_prefetch=1,
        grid=(rows,),
        in_specs=[row3(N_HEADS, D_ATTN), row3(n_sel * N_HEADS, HEAD_DIM), row3(n_sel * N_HEADS, HEAD_DIM),
                  row3(1, LANES), row3(1, LANES), row3(1, LANES),
                  per_b(LANES, D_ATTN), per_b(LANES, D_ATTN), row3(1, D_ATTN)],
        out_specs=row3(1, D_ATTN),
    )
    as_rows = lambda a: a.reshape(rows, 1, a.shape[-1])
    return pl.pallas_call(
        functools.partial(_sample_attn_kernel, past=past, n_sel=n_sel),
        out_shape=jax.ShapeDtypeStruct((rows, 1, D_ATTN), F32),
        grid_spec=grid_spec,
        compiler_params=_params("parallel"),
        name="sample_attn",
    )(cnt, q8, k_sel.reshape(rows, n_sel * N_HEADS, HEAD_DIM), v_sel.reshape(rows, n_sel * N_HEADS, HEAD_DIM),
      as_rows(scores_new), as_rows(thr), as_rows(j_max), k_new_pad, v_new_pad, as_rows(ga))


CONV_TC = 256
CONV_HALO = 32
CONV_RB = 32
CONV_CB = 512
CONV_SHIFT_ROWS = 40


def _conv_finish(y, cb, g, b, gate):
    y = _layer_norm(y + cb, g, b)
    return y * _sigmoid(y) * gate


def _conv_prompt_kernel(u_ref, prev_ref, head_ref, gc_ref, w_ref, cb_ref, g_ref, b_ref, o_ref, win_ref, sh_ref, y_ref):
    i = pl.program_id(1)

    @pl.when(i == 0)
    def _():
        win_ref[0:CONV_HALO, :] = head_ref[...]

    @pl.when(i > 0)
    def _():
        win_ref[0:CONV_HALO, :] = prev_ref[...]

    win_ref[CONV_HALO:, :] = u_ref[...]
    n_sh = sh_ref.shape[1]
    for s in range(1, SUBLANES):
        for r in range(0, n_sh, CONV_SHIFT_ROWS):
            sh_ref[s - 1, r:r + CONV_SHIFT_ROWS, :] = win_ref[r + s:r + s + CONV_SHIFT_ROWS, :]
    off = CONV_HALO - (CONV_W - 1)
    n_ch = u_ref.shape[1]
    for r0 in range(0, CONV_TC, CONV_RB):
        for c0 in range(0, n_ch, CONV_CB):
            cs = slice(c0, c0 + CONV_CB)
            acc = jnp.zeros((CONV_RB, CONV_CB), F32)
            for t in range(CONV_W):
                s, base = (off + t) % SUBLANES, r0 + (off + t) // SUBLANES * SUBLANES
                rows = win_ref[base:base + CONV_RB, cs] if s == 0 else sh_ref[s - 1, base:base + CONV_RB, cs]
                acc = acc + rows * w_ref[t:t + 1, cs]
            y_ref[r0:r0 + CONV_RB, cs] = acc
    for r0 in range(0, CONV_TC, CONV_RB):
        rows = slice(r0, r0 + CONV_RB)
        gate = gc_ref[rows, :].astype(F32)
        o_ref[rows, :] = _conv_finish(y_ref[rows, :], cb_ref[...], g_ref[...], b_ref[...], gate).astype(o_ref.dtype)


def _conv_prompt(u, head, gc, conv_w, conv_b, g, b, n_batch, seq):
    c = u.shape[-1]
    u3 = u.reshape(n_batch, seq, c)
    gc3 = gc.reshape(n_batch, seq, c)
    ratio = CONV_TC // CONV_HALO
    vec = lambda: pl.BlockSpec((1, c), lambda bb, i: (0, 0))
    out = pl.pallas_call(
        _conv_prompt_kernel,
        out_shape=jax.ShapeDtypeStruct((n_batch, seq, c), BF16),
        grid=(n_batch, seq // CONV_TC),
        in_specs=[pl.BlockSpec((None, CONV_TC, c), lambda bb, i: (bb, i, 0)),
                  pl.BlockSpec((None, CONV_HALO, c), lambda bb, i: (bb, jnp.maximum(i * ratio - 1, 0), 0)),
                  pl.BlockSpec((CONV_HALO, c), lambda bb, i: (0, 0)),
                  pl.BlockSpec((None, CONV_TC, c), lambda bb, i: (bb, i, 0)),
                  pl.BlockSpec((CONV_W, c), lambda bb, i: (0, 0)),
                  vec(), vec(), vec()],
        out_specs=pl.BlockSpec((None, CONV_TC, c), lambda bb, i: (bb, i, 0)),
        scratch_shapes=[pltpu.VMEM((CONV_HALO + CONV_TC, c), F32),
                        pltpu.VMEM((SUBLANES - 1, CONV_HALO + CONV_TC - SUBLANES, c), F32),
                        pltpu.VMEM((CONV_TC, c), F32)],
        compiler_params=_params("parallel", "arbitrary"),
        name="conv_prompt",
    )(u3, u3, head, gc3, conv_w, conv_b.reshape(1, c), g.reshape(1, c), b.reshape(1, c))
    return out.reshape(n_batch * seq, c)


def _conv_sample_kernel(st_ref, u_ref, gc_ref, w_ref, cb_ref, g_ref, b_ref, o_ref):
    n_hist = st_ref.shape[0]
    n_new = u_ref.shape[0]
    for r in range(n_new):
        acc = jnp.zeros(u_ref.shape[1:], F32)
        for t in range(CONV_W):
            src = r + t
            row = st_ref[src] if src < n_hist else u_ref[src - n_hist]
            acc = acc + row * w_ref[t:t + 1, :]
        o_ref[r] = _conv_finish(acc, cb_ref[...], g_ref[...], b_ref[...], gc_ref[r])


def _conv_sample(state_t, u_t, gc_t, conv_w, conv_b, g, b):
    c = u_t.shape[-1]
    return pl.pallas_call(
        _conv_sample_kernel,
        out_shape=jax.ShapeDtypeStruct(u_t.shape, F32),
        compiler_params=pltpu.CompilerParams(vmem_limit_bytes=VMEM_LIMIT),
        name="conv_sample",
    )(state_t, u_t, gc_t, conv_w, conv_b.reshape(1, c), g.reshape(1, c), b.reshape(1, c))


def _out_kernel(x_ref, a_ref, c_ref, wa_ref, wc_ref, gi_ref, bi_ref, go_ref, bo_ref, o_ref, *, alpha):
    h = _layer_norm(x_ref[...], gi_ref[...], bi_ref[...])
    z = jnp.dot(a_ref[...], wa_ref[...], preferred_element_type=F32)
    z = z + jnp.dot(c_ref[...], wc_ref[...], preferred_element_type=F32)
    o_ref[...] = _layer_norm(alpha * h + z, go_ref[...], bo_ref[...])


def _out_projection(x, a, c, wa, wc, gi, bi, go, bo, alpha, tm):
    m, d = x.shape
    row = lambda w: pl.BlockSpec((tm, w), lambda i: (i, 0))
    full = lambda arr: pl.BlockSpec(arr.shape, lambda i: (0, 0))
    vecs = [v.reshape(1, d) for v in (gi, bi, go, bo)]
    return pl.pallas_call(
        functools.partial(_out_kernel, alpha=alpha),
        out_shape=jax.ShapeDtypeStruct((m, d), F32),
        grid=(m // tm,),
        in_specs=[row(d), row(a.shape[1]), row(c.shape[1]), full(wa), full(wc)] + [full(v) for v in vecs],
        out_specs=row(d),
        compiler_params=_params("parallel"),
        name="out_proj",
    )(x, a, c, wa, wc, *vecs)


def _split_w_in(w):
    d_conv = (w.shape[1] - (3 * D_ATTN + N_IDX_HEADS * D_IDX + D_IDX + N_IDX_HEADS + D_ATTN)) // 3
    o = [0]
    for n in (D_ATTN, 2 * D_ATTN, N_IDX_HEADS * D_IDX, D_IDX + N_IDX_HEADS, D_ATTN, d_conv, d_conv, d_conv):
        o.append(o[-1] + n)
    piece = lambda i: w[:, o[i]:o[i + 1]].astype(BF16)
    kiwi = jnp.pad(w[:, o[3]:o[4]], ((0, 0), (0, 2 * LANES - (D_IDX + N_IDX_HEADS)))).astype(BF16)
    return dict(q=piece(0), kv=piece(1), qi=piece(2), kiwi=kiwi, ga=piece(4),
                glu_a=piece(5), glu_b=piece(6), gc=piece(7))


def kernel(x_prompt, x_sample, cache_k, cache_v, cache_kidx, state_conv, page_table, meta_tokens,
           ln_in_g, ln_in_b, w_in, ln_kidx_g, ln_kidx_b, conv_w, conv_b, ln_conv_g, ln_conv_b,
           w_out, ln_out_g, ln_out_b):
    n_batch, seq, d_model = x_prompt.shape
    db, n_new, _ = x_sample.shape
    depth = w_in.shape[0]
    assert depth == 1, "one mixer layer per step"
    assert seq >= CONV_W - 1
    n_pages = page_table.shape[1]
    past = n_pages * PAGE_SIZE
    n_sel_prompt = min(TOPK_MAX, seq // 4)
    n_sel_sample = min(TOPK_MAX, (past + n_new) // 4)
    alpha = (2.0 * depth) ** 0.25
    l = 0

    w = _split_w_in(w_in[l])
    d_conv = w["gc"].shape[1]
    wo = w_out[l].astype(BF16)
    wo_a, wo_c = wo[:D_ATTN], wo[D_ATTN:]

    xp = x_prompt.reshape(n_batch * seq, d_model)
    xs = jnp.concatenate([meta_tokens.astype(F32), x_sample.reshape(db * n_new, d_model)], axis=0)
    n_small = xs.shape[0]
    hp = _ln_cast(xp, ln_in_g, ln_in_b, 512)
    hs = _ln_cast(xs, ln_in_g, ln_in_b, n_small)
    pp = _in_projection(hp, w, ln_kidx_g[l], ln_kidx_b[l], 512)
    ps = _in_projection(hs, w, ln_kidx_g[l], ln_kidx_b[l], n_small)
    pm = {n: a[:N_META] for n, a in ps.items()}
    ps = {n: a[N_META:] for n, a in ps.items()}

    pad_rows = lambda a: jnp.pad(a, ((0, LANES - a.shape[0]), (0, 0)))
    meta = dict(ki=pad_rows(pm["ki16"]), k=pad_rows(pm["k16"]), vt=jnp.transpose(pad_rows(pm["v16"])))
    a_p = _prompt_attention(pp, meta, n_batch, seq, n_sel_prompt).reshape(n_batch * seq, D_ATTN)
    head = jnp.concatenate([jnp.zeros((CONV_HALO - N_META, d_conv), F32), pm["u"]], axis=0)
    c_p = _conv_prompt(pp["u"], head, pp["gc"], conv_w[l], conv_b[l], ln_conv_g[l], ln_conv_b[l], n_batch, seq)
    y_p = _out_projection(xp, a_p, c_p, wo_a, wo_c, ln_in_g, ln_in_b, ln_out_g[l], ln_out_b[l], alpha, 256)
    y_prompt = y_p.reshape(n_batch, seq, d_model)

    def with_meta(m_rows, p_rows):
        m_b = jnp.broadcast_to(m_rows[None], (n_batch,) + m_rows.shape)
        return jnp.concatenate([m_b, p_rows.reshape(n_batch, seq, -1)], axis=1)

    new_k_p = with_meta(pm["k32"], pp["k32"]).reshape(1, n_batch, N_META + seq, N_HEADS, HEAD_DIM)
    new_v_p = with_meta(pm["v32"], pp["v32"]).reshape(1, n_batch, N_META + seq, N_HEADS, HEAD_DIM)
    new_ki_p = with_meta(pm["ki32"], pp["ki32"])[None]
    new_conv_p = pp["u"].reshape(n_batch, seq, d_conv)[:, -(CONV_W - 1):][None]

    qi_rows = ps["qi"].reshape(db, n_new * N_IDX_HEADS, D_IDX)
    wcol = jnp.broadcast_to(ps["wi"][:, :N_IDX_HEADS].reshape(db, n_new * N_IDX_HEADS, 1),
                            (db, n_new * N_IDX_HEADS, LANES))
    pad_new = lambda a: jnp.pad(a.reshape(db, n_new, a.shape[-1]), ((0, 0), (0, LANES - n_new), (0, 0)))
    sc_past, sc_new = _sample_scores(page_table, qi_rows, wcol, pad_new(ps["ki16"]), cache_kidx, l)
    thr, j_max = _sample_threshold(sc_past, sc_new, n_sel_sample)
    key_rows = lambda c: c.reshape(-1, N_HEADS, HEAD_DIM)
    n_pool = cache_k.shape[1]
    k_sel, v_sel, cnt = _sample_gather(sc_past, thr, j_max, page_table + l * n_pool,
                                       key_rows(cache_k), key_rows(cache_v), n_sel_sample)
    q4 = ps["q"].reshape(db * n_new, N_HEADS, HEAD_DIM)
    q8 = jnp.einsum("rhd,hg->rhgd", q4, jnp.eye(N_HEADS, dtype=BF16)).reshape(db * n_new, N_HEADS, D_ATTN)
    a_s = _sample_attention(cnt.reshape(db * n_new, SC_LANES)[:, 0], q8, k_sel, v_sel, sc_new, thr, j_max,
                            pad_new(ps["k16"]), pad_new(ps["v16"]),
                            ps["ga"].astype(F32), n_new, past, n_sel_sample)
    a_s = a_s.reshape(db * n_new, D_ATTN).astype(BF16)

    to_t = lambda a: jnp.transpose(a.reshape(db, -1, d_conv), (1, 0, 2))
    c_t = _conv_sample(to_t(state_conv[l].astype(F32)), to_t(ps["u"]), to_t(ps["gc"].astype(F32)),
                       conv_w[l], conv_b[l], ln_conv_g[l], ln_conv_b[l])
    c_s = jnp.transpose(c_t, (1, 0, 2)).reshape(db * n_new, d_conv).astype(BF16)
    y_s = _out_projection(x_sample.reshape(db * n_new, d_model), a_s, c_s, wo_a, wo_c,
                          ln_in_g, ln_in_b, ln_out_g[l], ln_out_b[l], alpha, db * n_new)
    y_sample = y_s.reshape(db, n_new, d_model)

    new_k_s = ps["k32"].reshape(1, db, n_new, N_HEADS, HEAD_DIM)
    new_v_s = ps["v32"].reshape(1, db, n_new, N_HEADS, HEAD_DIM)
    new_ki_s = ps["ki32"].reshape(1, db, n_new, D_IDX)
    u_ext_s = jnp.concatenate([state_conv[l].astype(F32), ps["u"].reshape(db, n_new, d_conv)], axis=1)
    new_conv_s = u_ext_s[:, -(CONV_W - 1):][None]

    return (y_prompt, y_sample, new_k_p, new_v_p, new_ki_p, new_conv_p,
            new_k_s, new_v_s, new_ki_s, new_conv_s)
```

```python
import functools

import jax
import jax.numpy as jnp
from jax import lax
from jax.experimental import pallas as pl
from jax.experimental.pallas import tpu as pltpu
from jax.experimental.pallas import tpu_sc as plsc

N_META = 16
N_HEADS = 8
HEAD_DIM = 128
D_ATTN = N_HEADS * HEAD_DIM
N_IDX_HEADS = 16
D_IDX = 128
TOPK_MAX = 256
CONV_W = 31
PAGE_SIZE = 128
LN_EPS = 1e-5
ATTN_SCALE = HEAD_DIM ** -0.5
IDX_SCALE = (N_IDX_HEADS * D_IDX) ** -0.5

LANES = 128
SUBLANES = 8
NEG_BIG = -1e30
F32_MAX = 3.4028235e38
INT_MAX = 2 ** 31 - 1
SEARCH_CAP = 400
VMEM_LIMIT = 56 * 1024 * 1024

F32 = jnp.float32
BF16 = jnp.bfloat16
I32 = jnp.int32


def _params(*sem):
    return pltpu.CompilerParams(dimension_semantics=sem, vmem_limit_bytes=VMEM_LIMIT)


def _layer_norm(x, g, b):
    mu = jnp.mean(x, axis=-1, keepdims=True)
    xc = x - mu
    var = jnp.mean(xc * xc, axis=-1, keepdims=True)
    return xc * lax.rsqrt(var + LN_EPS) * g + b


def _sigmoid(z):
    return 1.0 / (1.0 + jnp.exp(-z))


def _dot_t(a, b):
    return lax.dot_general(a, b, (((1,), (1,)), ((), ())), preferred_element_type=F32)


def _fold_rows(x, reduce):
    return reduce(x.reshape(x.shape[0] // SUBLANES, SUBLANES, x.shape[1]), axis=0)


def _col_reduce(x, reduce):
    return reduce(_fold_rows(x, reduce), axis=0, keepdims=True)


def _topk_threshold(count, lo0, hi0, c0, active, k, n_total):
    one = jnp.ones(lo0.shape, I32)

    def cond(st):
        it, _, _, c_lo, stalled = st
        waiting = jnp.sum(jnp.where(active & (c_lo != k) & (stalled == 0), one, 0))
        return (it < SEARCH_CAP) & (waiting > 0)

    def step(st):
        it, lo, hi, c_lo, stalled = st
        mid = 0.5 * lo + 0.5 * hi
        stall = (mid <= lo) | (mid >= hi)
        c = count(lambda s, i: s >= mid)
        up = (c >= k) & jnp.logical_not(stall)
        down = (c < k) & jnp.logical_not(stall)
        return (it + 1, jnp.where(up, mid, lo), jnp.where(down, mid, hi), jnp.where(up, c, c_lo),
                jnp.where(stall, one, stalled))

    def body(st):
        return step(step(st))

    _, thr, _, c_lo, _ = lax.while_loop(cond, body, (jnp.int32(0), lo0, hi0, c0, jnp.zeros(lo0.shape, I32)))
    tied = active & (c_lo > k)

    def break_ties(_):
        need = k - count(lambda s, i: s > thr)

        def step(_, st):
            j_lo, j_hi = st
            j_mid = j_lo + (j_hi - j_lo) // 2
            ok = count(lambda s, i: (s == thr) & (i <= j_mid)) >= need
            return jnp.where(ok, j_lo, j_mid), jnp.where(ok, j_mid, j_hi)

        n_steps = (n_total + 1).bit_length()
        _, j_hi = lax.fori_loop(0, n_steps, step, (jnp.full(lo0.shape, -1, I32), jnp.full(lo0.shape, n_total, I32)))
        return jnp.where(tied, j_hi, INT_MAX)

    j_max = lax.cond(jnp.sum(jnp.where(tied, one, 0)) > 0, break_ties,
                     lambda _: jnp.full(lo0.shape, INT_MAX, I32), 0)
    thr = jnp.where(active, thr, -F32_MAX)
    return thr, j_max


def _above(x):
    return x + (jnp.abs(x) * 2.0 ** -20 + 1e-30)


def _silu(z):
    return z * _sigmoid(z)


def _cast_once(raw_refs, bf16_refs):
    @pl.when(pl.program_id(0) == 0)
    def _():
        for raw, dst in zip(raw_refs, bf16_refs):
            dst[...] = raw[...].astype(dst.dtype)


def _proj_ln_q_ga_kernel(x_ref, wq32_ref, wga_ref, g_ref, b_ref, h_ref, q_ref, ga_ref, wq_ref):
    _cast_once([wq32_ref], [wq_ref])
    h = _layer_norm(x_ref[...], g_ref[...], b_ref[...]).astype(BF16)
    h_ref[...] = h
    q_ref[...] = (_dot_t(h, wq_ref[...]) * ATTN_SCALE).astype(q_ref.dtype)
    ga_ref[...] = _silu(_dot_t(h, wga_ref[...])).astype(ga_ref.dtype)


def _proj_kv_kernel(h_ref, wk32_ref, wv32_ref, k32_ref, v32_ref, k16_ref, v16_ref, wk_ref, wv_ref):
    _cast_once([wk32_ref, wv32_ref], [wk_ref, wv_ref])
    h = h_ref[...]
    k = _dot_t(h, wk_ref[...])
    v = _dot_t(h, wv_ref[...])
    k32_ref[...] = k
    v32_ref[...] = v
    k16_ref[...] = k.astype(k16_ref.dtype)
    v16_ref[...] = v.astype(v16_ref.dtype)


def _proj_glu_gc_kernel(h_ref, wa_ref, wb_ref, wgc_ref, u_ref, gc_ref):
    h = h_ref[...]
    u_ref[...] = _dot_t(h, wa_ref[...]) * _sigmoid(_dot_t(h, wb_ref[...]))
    gc_ref[...] = _silu(_dot_t(h, wgc_ref[...])).astype(gc_ref.dtype)


def _proj_qi_kiwi_kernel(h_ref, wqa32_ref, wqb32_ref, wkiwi32_ref, g_ref, b_ref, qi_ref, ki32_ref, ki16_ref, wi_ref,
                         wqa_ref, wqb_ref, wkiwi_ref):
    _cast_once([wqa32_ref, wqb32_ref, wkiwi32_ref], [wqa_ref, wqb_ref, wkiwi_ref])
    h = h_ref[...]
    half = wqa_ref.shape[0]
    qi_ref[:, :half] = _dot_t(h, wqa_ref[...]).astype(qi_ref.dtype)
    qi_ref[:, half:] = _dot_t(h, wqb_ref[...]).astype(qi_ref.dtype)
    z = _dot_t(h, wkiwi_ref[...])
    ki = _layer_norm(z[:, :D_IDX], g_ref[...], b_ref[...])
    ki32_ref[...] = ki
    ki16_ref[...] = ki.astype(ki16_ref.dtype)
    lane = lax.broadcasted_iota(I32, wi_ref.shape, 1)
    wi_ref[...] = jnp.where(lane < N_IDX_HEADS, z[:, D_IDX:] * IDX_SCALE, 0.0)


def _proj(kernel, h, raw, ws, extra, outs, tm, name):
    m, d = h.shape
    w_t, cols = raw
    in_specs = [pl.BlockSpec((tm, d), lambda i: (i, 0))]
    in_specs += [pl.BlockSpec((width, d), lambda i, c=c: (c, 0), pipeline_mode=pl.Buffered(1))
                 for width, c in cols]
    in_specs += [pl.BlockSpec((rows, d), lambda i, c=c: (c, 0)) for _, rows, c in ws]
    in_specs += [pl.BlockSpec(e.shape, lambda i: (0, 0)) for e in extra]
    out_shape = [jax.ShapeDtypeStruct((m, n), dt) for n, dt in outs]
    out_specs = [pl.BlockSpec((tm, n), lambda i: (i, 0)) for n, _ in outs]
    return pl.pallas_call(
        kernel,
        out_shape=out_shape,
        grid=(m // tm,),
        in_specs=in_specs,
        out_specs=out_specs,
        scratch_shapes=[pltpu.VMEM((width, d), BF16) for width, _ in cols],
        compiler_params=_params("arbitrary"),
        name=name,
    )(h, *([w_t] * len(cols)), *[w for w, _, _ in ws], *extra)


def _in_projection(x, ln_g, ln_b, w_t, w, kidx_g, kidx_b, tm):
    d = x.shape[1]
    d_conv = w["d_conv"]
    rest = lambda i: (w["rest"], d_conv, i)
    assert D_ATTN == d_conv
    kiwi_w = 2 * LANES
    kiwi_col = 3 * D_ATTN + N_IDX_HEADS * D_IDX
    assert kiwi_col % kiwi_w == 0 and D_IDX + N_IDX_HEADS <= kiwi_w
    raw = lambda *cols: (w_t, list(cols))
    h, q, ga = _proj(_proj_ln_q_ga_kernel, x, raw((D_ATTN, 0)), [rest(0)],
                     [ln_g.reshape(1, d), ln_b.reshape(1, d)],
                     [(d, BF16), (D_ATTN, BF16), (D_ATTN, BF16)], tm, "proj_ln_q_gattn")
    k32, v32, k16, v16 = _proj(_proj_kv_kernel, h, raw((D_ATTN, 1), (D_ATTN, 2)), [], [],
                               [(D_ATTN, F32), (D_ATTN, F32), (D_ATTN, BF16), (D_ATTN, BF16)], tm, "proj_kv")
    qi, ki32, ki16, wi = _proj(_proj_qi_kiwi_kernel, h,
                               raw((D_ATTN, 3), (D_ATTN, 4), (kiwi_w, kiwi_col // kiwi_w)), [],
                               [kidx_g.reshape(1, D_IDX), kidx_b.reshape(1, D_IDX)],
                               [(N_IDX_HEADS * D_IDX, BF16), (D_IDX, F32), (D_IDX, BF16), (LANES, F32)],
                               tm, "proj_qi_kiwi")
    u, gc = _proj(_proj_glu_gc_kernel, h, raw(), [rest(1), rest(2), rest(3)], [],
                  [(d_conv, F32), (d_conv, BF16)], tm, "proj_glu_gconv")
    return dict(q=q, k32=k32, v32=v32, k16=k16, v16=v16, qi=qi, ki32=ki32, ki16=ki16, wi=wi, ga=ga, u=u, gc=gc)


TQ = 256
KC = 256


def _prompt_attn_kernel(qi_ref, wi_ref, q_ref, ga_ref, kip_ref, kp_ref, vtp_ref, kim_ref, km_ref, vtm_ref,
                        o_ref, sc_ref, scm_ref, wt_ref, s_ref, m_ref, l_ref, acc_ref, *, n_sel):
    qb = pl.program_id(1)
    n_chunks = (qb * TQ + TQ - 1) // KC + 1
    q_pos = qb * TQ + lax.broadcasted_iota(I32, (1, TQ), 1)
    row_kc = lax.broadcasted_iota(I32, (KC, TQ), 0)
    row_m = lax.broadcasted_iota(I32, (LANES, TQ), 0)

    wt_ref[...] = jnp.transpose(wi_ref[...])

    def idx_scores(ki):
        acc = jnp.zeros((ki.shape[0], TQ), F32)
        for h in range(N_IDX_HEADS):
            s = _dot_t(ki, qi_ref[:, h * D_IDX:(h + 1) * D_IDX])
            acc = acc + jnp.maximum(s, 0.0) * wt_ref[h:h + 1, :]
        return acc

    sm = idx_scores(kim_ref[...])
    meta_vis = row_m < N_META
    scm_ref[...] = jnp.where(meta_vis, sm, -jnp.inf)

    def score_chunk(j, carry):
        rmax, rmin = carry
        r0 = pl.multiple_of(j * KC, KC)
        s = idx_scores(kip_ref[pl.ds(r0, KC), :])
        vis = (j * KC + row_kc) <= q_pos
        s_vis = jnp.where(vis, s, -jnp.inf)
        sc_ref[j] = s_vis
        rmax = jnp.maximum(rmax, _fold_rows(s_vis, jnp.max))
        rmin = jnp.minimum(rmin, _fold_rows(jnp.where(vis, s, jnp.inf), jnp.min))
        return rmax, rmin

    rmax, rmin = lax.fori_loop(0, n_chunks, score_chunk,
                               (_fold_rows(jnp.where(meta_vis, sm, -jnp.inf), jnp.max),
                                _fold_rows(jnp.where(meta_vis, sm, jnp.inf), jnp.min)))

    def count(f):
        def ones(mask):
            return _fold_rows(jnp.where(mask, 1, 0), jnp.sum)

        def body(j, c):
            return c + ones(f(sc_ref[j], LANES + j * KC + row_kc))

        c = lax.fori_loop(0, n_chunks, body, ones(f(scm_ref[...], row_m)))
        return jnp.sum(c, axis=0, keepdims=True)

    n_vis = N_META + 1 + q_pos
    thr, j_max = _topk_threshold(count, jnp.min(rmin, axis=0, keepdims=True),
                                 _above(jnp.max(rmax, axis=0, keepdims=True)), n_vis,
                                 n_vis > n_sel, n_sel, LANES + sc_ref.shape[0] * KC)

    def bias_of(s, idx):
        keep = (s > thr) | ((s == thr) & (idx <= j_max))
        return jnp.where(keep, 0.0, NEG_BIG)

    m_ref[...] = jnp.full(m_ref.shape, NEG_BIG, F32)
    l_ref[...] = jnp.zeros(l_ref.shape, F32)
    acc_ref[...] = jnp.zeros(acc_ref.shape, F32)

    def attend(bias, k_blk, vt_blk):
        n = bias.shape[0]
        heads = [slice(h * HEAD_DIM, (h + 1) * HEAD_DIM) for h in range(N_HEADS)]
        m_new = []
        for h, cs in enumerate(heads):
            s = _dot_t(k_blk[:, cs], q_ref[:, cs]) + bias
            s_ref[h, 0:n, :] = s
            m_new.append(jnp.maximum(m_ref[h], _col_reduce(s, jnp.max)))
        for h, cs in enumerate(heads):
            alpha = jnp.exp(m_ref[h] - m_new[h])
            p = jnp.exp(s_ref[h, 0:n, :] - m_new[h])
            l_ref[h] = alpha * l_ref[h] + _col_reduce(p, jnp.sum)
            pv = jnp.dot(vt_blk[cs, :], p.astype(BF16), preferred_element_type=F32)
            acc_ref[cs, :] = alpha * acc_ref[cs, :] + pv
            m_ref[h] = m_new[h]

    attend(bias_of(scm_ref[...], row_m), km_ref[...], vtm_ref[...])

    def attn_chunk(j, carry):
        r0 = pl.multiple_of(j * KC, KC)
        attend(bias_of(sc_ref[j], LANES + j * KC + row_kc), kp_ref[pl.ds(r0, KC), :], vtp_ref[j])
        return carry

    lax.fori_loop(0, n_chunks, attn_chunk, 0)

    for h in range(N_HEADS):
        cs = slice(h * HEAD_DIM, (h + 1) * HEAD_DIM)
        out = jnp.transpose(acc_ref[cs, :] / l_ref[h])
        o_ref[:, cs] = (out * ga_ref[:, cs].astype(F32)).astype(o_ref.dtype)


def _prompt_attention(p, meta, n_batch, seq, n_sel):
    def rows(a):
        return a.reshape(n_batch, seq, a.shape[-1])

    qi, wi, q, ga, ki16, k16 = (rows(p[n]) for n in ("qi", "wi", "q", "ga", "ki16", "k16"))
    n_kc = seq // KC
    vt = jnp.transpose(p["v16"].reshape(n_batch, n_kc, KC, D_ATTN), (0, 1, 3, 2))
    blk = lambda w: pl.BlockSpec((None, TQ, w), lambda b, i: (b, i, 0))
    res = lambda w: pl.BlockSpec((None, seq, w), lambda b, i: (b, 0, 0), pipeline_mode=pl.Buffered(1))
    const = lambda a: pl.BlockSpec(a.shape, lambda b, i: (0, 0), pipeline_mode=pl.Buffered(1))
    return pl.pallas_call(
        functools.partial(_prompt_attn_kernel, n_sel=n_sel),
        out_shape=jax.ShapeDtypeStruct((n_batch, seq, D_ATTN), BF16),
        grid=(n_batch, seq // TQ),
        in_specs=[blk(N_IDX_HEADS * D_IDX), blk(LANES), blk(D_ATTN), blk(D_ATTN),
                  res(D_IDX), res(D_ATTN),
                  pl.BlockSpec((None, n_kc, D_ATTN, KC), lambda b, i: (b, 0, 0, 0), pipeline_mode=pl.Buffered(1)),
                  const(meta["ki"]), const(meta["k"]), const(meta["vt"])],
        out_specs=blk(D_ATTN),
        scratch_shapes=[pltpu.VMEM((n_kc, KC, TQ), F32),
                        pltpu.VMEM((LANES, TQ), F32),
                        pltpu.VMEM((LANES, TQ), F32),
                        pltpu.VMEM((N_HEADS, KC, TQ), F32),
                        pltpu.VMEM((N_HEADS, 1, TQ), F32),
                        pltpu.VMEM((N_HEADS, 1, TQ), F32),
                        pltpu.VMEM((D_ATTN, TQ), F32)],
        compiler_params=_params("parallel", "arbitrary"),
        name="prompt_attn",
    )(qi, wi, q, ga, ki16, k16, vt, meta["ki"], meta["k"], meta["vt"])


SCORE_PAGES = 32


def _idx_rows(qi, wcol, ki, n_q):
    a = jnp.maximum(_dot_t(qi, ki), 0.0) * wcol
    return jnp.sum(a.reshape(n_q, N_IDX_HEADS, ki.shape[0]), axis=1)


def _sample_score_kernel(pt_ref, qi_ref, wcol_ref, kin_ref, *refs):
    pages, (o_ref, on_ref) = refs[:SCORE_PAGES], refs[SCORE_PAGES:]
    n_q = o_ref.shape[0]
    qi = qi_ref[...]
    wcol = wcol_ref[...]
    for r, page in enumerate(pages):
        o_ref[:, r * PAGE_SIZE:(r + 1) * PAGE_SIZE] = _idx_rows(qi, wcol, page[...].astype(BF16), n_q)

    @pl.when(pl.program_id(1) == pl.num_programs(1) - 1)
    def _():
        sc = _idx_rows(qi, wcol, kin_ref[...], n_q)
        lane = lax.broadcasted_iota(I32, (n_q, LANES), 1)
        row = lax.broadcasted_iota(I32, (n_q, LANES), 0)
        on_ref[...] = jnp.where(lane <= row, sc, -jnp.inf)


def _sample_scores(page_table, qi_rows, wcol, ki_new_pad, cache_kidx, layer):
    db, n_pages = page_table.shape
    n_rows = qi_rows.shape[1]
    n_q = n_rows // N_IDX_HEADS
    steps = n_pages // SCORE_PAGES
    per_b = lambda *shape: pl.BlockSpec((None,) + shape, lambda b, j, pt: (b, 0, 0))
    page_spec = lambda r: pl.BlockSpec((None, None, PAGE_SIZE, D_IDX),
                                       lambda b, j, pt, r=r: (layer, pt[b, j * SCORE_PAGES + r], 0, 0))
    grid_spec = pltpu.PrefetchScalarGridSpec(
        num_scalar_prefetch=1,
        grid=(db, steps),
        in_specs=[per_b(n_rows, D_IDX), per_b(n_rows, LANES), per_b(LANES, D_IDX)]
                 + [page_spec(r) for r in range(SCORE_PAGES)],
        out_specs=[pl.BlockSpec((None, n_q, SCORE_PAGES * PAGE_SIZE), lambda b, j, pt: (b, 0, j)),
                   per_b(n_q, LANES)],
    )
    return pl.pallas_call(
        _sample_score_kernel,
        out_shape=[jax.ShapeDtypeStruct((db, n_q, n_pages * PAGE_SIZE), F32),
                   jax.ShapeDtypeStruct((db, n_q, LANES), F32)],
        grid_spec=grid_spec,
        compiler_params=_params("parallel", "arbitrary"),
        name="sample_scores",
    )(page_table, qi_rows, wcol, ki_new_pad, *([cache_kidx] * SCORE_PAGES))


def _sample_thr_kernel(sp_ref, sn_ref, thr_ref, j_ref, *, n_sel, n_q):
    rows, past = sp_ref.shape
    lane = lax.broadcasted_iota(I32, (rows, LANES), 1)
    idx_p = lax.broadcasted_iota(I32, (rows, past), 1)
    sn = sn_ref[...]

    def count(f):
        c = jnp.sum(jnp.where(f(sp_ref[...], idx_p), 1, 0), axis=1, keepdims=True)
        return c + jnp.sum(jnp.where(f(sn, past + lane), 1, 0), axis=1, keepdims=True)

    new_vis = sn > -jnp.inf
    rmax = jnp.maximum(jnp.max(sp_ref[...], axis=1, keepdims=True), jnp.max(sn, axis=1, keepdims=True))
    rmin = jnp.minimum(jnp.min(sp_ref[...], axis=1, keepdims=True),
                       jnp.min(jnp.where(new_vis, sn, jnp.inf), axis=1, keepdims=True))
    n_vis = past + 1 + lax.rem(lax.broadcasted_iota(I32, (rows, 1), 0), n_q)
    thr, j_max = _topk_threshold(count, rmin, _above(rmax), n_vis, n_vis > n_sel, n_sel, past + LANES)
    thr_ref[...] = jnp.broadcast_to(thr, thr_ref.shape)
    j_ref[...] = jnp.broadcast_to(j_max, j_ref.shape)


def _sample_threshold(scores_past, scores_new, n_sel):
    db, n_q, past = scores_past.shape
    rows = db * n_q
    out = jax.ShapeDtypeStruct((rows, LANES), F32), jax.ShapeDtypeStruct((rows, LANES), I32)
    thr, j_max = pl.pallas_call(
        functools.partial(_sample_thr_kernel, n_sel=n_sel, n_q=n_q),
        out_shape=out,
        compiler_params=pltpu.CompilerParams(vmem_limit_bytes=VMEM_LIMIT),
        name="sample_threshold",
    )(scores_past.reshape(rows, past), scores_new.reshape(rows, LANES))
    return thr.reshape(db, n_q, LANES), j_max.reshape(db, n_q, LANES)


SC_LANES = 16
SC_WORKERS = 32
GATHER_ROWS = 32


def _sc_gather_kernel(sc_hbm, thr_hbm, jmax_hbm, pt_hbm, kc_hbm, vc_hbm, ksel_hbm, vsel_hbm, cnt_hbm,
                      sc_v, thr_v, jmax_v, pt_v, list_v, cnt_v, buf0, buf1, sem0, sem1,
                      *, n_q, past, n_pages, n_sel, rows_per_worker, n_cores):
    wid = lax.axis_index("s") * n_cores + lax.axis_index("c")
    lanes = lax.iota(I32, SC_LANES)
    zero = jnp.zeros((SC_LANES,), I32)
    bufs, sems = (buf0, buf1), (sem0, sem1)

    @pl.loop(0, rows_per_worker)
    def _(t):
        r = wid * rows_per_worker + t
        pltpu.sync_copy(pt_hbm.at[pl.ds((r // n_q) * n_pages, n_pages)], pt_v)
        pltpu.sync_copy(sc_hbm.at[pl.ds(r * past, past)], sc_v)
        pltpu.sync_copy(thr_hbm.at[pl.ds(r * LANES, SC_LANES)], thr_v)
        pltpu.sync_copy(jmax_hbm.at[pl.ds(r * LANES, SC_LANES)], jmax_v)
        thr = thr_v[...]
        j_max = jmax_v[...]
        for i in range(n_sel // SC_LANES):
            list_v[pl.ds(i * SC_LANES, SC_LANES)] = zero

        def body(i, cnt):
            base = i * SC_LANES
            s = sc_v[pl.ds(base, SC_LANES)]
            idx = base + lanes
            keep = (s > thr) | ((s == thr) & (idx <= j_max))
            page = plsc.load_gather(pt_v, [lax.shift_right_logical(idx, 7)])
            phys = page * PAGE_SIZE + (idx & (PAGE_SIZE - 1))
            pos = cnt + plsc.cumsum(jnp.where(keep, 1, 0)) - 1
            keep = keep & (pos < n_sel)
            plsc.store_scatter(list_v, [pos], phys, mask=keep)
            return cnt + plsc.all_reduce_population_count(keep)

        cnt = lax.fori_loop(0, past // SC_LANES, body, zero)
        cnt_v[...] = cnt
        pltpu.sync_copy(cnt_v, cnt_hbm.at[pl.ds(r * SC_LANES, SC_LANES)])

        n_pieces = n_sel // GATHER_ROWS
        jobs = [(kc_hbm, ksel_hbm, p) for p in range(n_pieces)] + [(vc_hbm, vsel_hbm, p) for p in range(n_pieces)]

        def gather(job, slot):
            src, _, p = job
            return pltpu.async_copy(src.at[list_v.at[pl.ds(p * GATHER_ROWS, GATHER_ROWS)]], bufs[slot], sems[slot])

        pending = gather(jobs[0], 0)
        for n, job in enumerate(jobs):
            pending.wait()
            if n + 1 < len(jobs):
                pending = gather(jobs[n + 1], (n + 1) % 2)
            _, dst, p = job
            pltpu.sync_copy(bufs[n % 2], dst.at[pl.ds(r * n_sel + p * GATHER_ROWS, GATHER_ROWS)])


def _sample_gather(scores_past, thr, j_max, page_table, cache_k_rows, cache_v_rows, n_sel):
    db, n_q, past = scores_past.shape
    rows = db * n_q
    n_pages = page_table.shape[1]
    info = plsc.get_sparse_core_info()
    assert info.num_lanes == SC_LANES and info.num_cores * info.num_subcores == SC_WORKERS
    assert rows % SC_WORKERS == 0 and n_sel % GATHER_ROWS == 0 and past % SC_LANES == 0
    mesh = plsc.VectorSubcoreMesh(core_axis_name="c", subcore_axis_name="s")
    sel = jax.ShapeDtypeStruct((rows * n_sel, N_HEADS, HEAD_DIM), F32)
    body = functools.partial(_sc_gather_kernel, n_q=n_q, past=past, n_pages=n_pages, n_sel=n_sel,
                             rows_per_worker=rows // SC_WORKERS, n_cores=info.num_cores)
    return pl.kernel(
        body,
        out_type=[sel, sel, jax.ShapeDtypeStruct((rows * SC_LANES,), I32)],
        mesh=mesh,
        scratch_types=[pltpu.VMEM((past,), F32), pltpu.VMEM((SC_LANES,), F32), pltpu.VMEM((SC_LANES,), I32),
                       pltpu.VMEM((n_pages,), I32), pltpu.VMEM((n_sel,), I32), pltpu.VMEM((SC_LANES,), I32),
                       pltpu.VMEM((GATHER_ROWS, N_HEADS, HEAD_DIM), F32),
                       pltpu.VMEM((GATHER_ROWS, N_HEADS, HEAD_DIM), F32),
                       pltpu.SemaphoreType.DMA, pltpu.SemaphoreType.DMA],
        compiler_params=pltpu.CompilerParams(needs_layout_passes=False),
        name="sample_gather",
    )(scores_past.reshape(-1), thr.reshape(-1), j_max.reshape(-1), page_table.reshape(-1),
      cache_k_rows, cache_v_rows)


def _sample_attn_kernel(cnt_ref, q_ref, k_ref, v_ref, sn_ref, thr_ref, jmax_ref, kn_ref, vn_ref, ga_ref, o_ref,
                        *, past, n_sel):
    r = pl.program_id(0)
    q8 = q_ref[...]

    def flat(rows_ref):
        heads = [rows_ref[pl.ds(h, n_sel, stride=N_HEADS), :] for h in range(N_HEADS)]
        return jnp.concatenate(heads, axis=1).astype(BF16)

    slot = lax.broadcasted_iota(I32, (N_HEADS, n_sel), 1)
    s_p = jnp.where(slot < cnt_ref[r], _dot_t(q8, flat(k_ref)), NEG_BIG)
    sn = sn_ref[...]
    idx_n = past + lax.broadcasted_iota(I32, sn.shape, 1)
    thr = thr_ref[:, 0:1]
    keep_n = (sn > thr) | ((sn == thr) & (idx_n <= jmax_ref[:, 0:1]))
    s_n = _dot_t(q8, kn_ref[...]) + jnp.where(keep_n, 0.0, NEG_BIG)

    m = jnp.maximum(jnp.max(s_p, axis=1, keepdims=True), jnp.max(s_n, axis=1, keepdims=True))
    p_p = jnp.exp(s_p - m)
    p_n = jnp.exp(s_n - m)
    den = jnp.sum(p_p, axis=1, keepdims=True) + jnp.sum(p_n, axis=1, keepdims=True)
    out8 = jnp.dot(p_p.astype(BF16), flat(v_ref), preferred_element_type=F32)
    out8 = (out8 + jnp.dot(p_n.astype(BF16), vn_ref[...], preferred_element_type=F32)) / den
    row = lax.broadcasted_iota(I32, out8.shape, 0)
    col_head = lax.broadcasted_iota(I32, out8.shape, 1) // HEAD_DIM
    o_ref[...] = jnp.sum(jnp.where(row == col_head, out8, 0.0), axis=0, keepdims=True) * ga_ref[...]


def _sample_attention(cnt, q8, k_sel, v_sel, scores_new, thr, j_max, k_new_pad, v_new_pad, ga, n_q, past, n_sel):
    rows = q8.shape[0]
    row3 = lambda *shape: pl.BlockSpec((None,) + shape, lambda r, c: (r, 0, 0))
    per_b = lambda *shape: pl.BlockSpec((None,) + shape, lambda r, c: (r // n_q, 0, 0))
    grid_spec = pltpu.PrefetchScalarGridSpec(
        num_scalar_prefetch=1,
        grid=(rows,),
        in_specs=[row3(N_HEADS, D_ATTN), row3(n_sel * N_HEADS, HEAD_DIM), row3(n_sel * N_HEADS, HEAD_DIM),
                  row3(1, LANES), row3(1, LANES), row3(1, LANES),
                  per_b(LANES, D_ATTN), per_b(LANES, D_ATTN), row3(1, D_ATTN)],
        out_specs=row3(1, D_ATTN),
    )
    as_rows = lambda a: a.reshape(rows, 1, a.shape[-1])
    return pl.pallas_call(
        functools.partial(_sample_attn_kernel, past=past, n_sel=n_sel),
        out_shape=jax.ShapeDtypeStruct((rows, 1, D_ATTN), F32),
        grid_spec=grid_spec,
        compiler_params=_params("parallel"),
        name="sample_attn",
    )(cnt, q8, k_sel.reshape(rows, n_sel * N_HEADS, HEAD_DIM), v_sel.reshape(rows, n_sel * N_HEADS, HEAD_DIM),
      as_rows(scores_new), as_rows(thr), as_rows(j_max), k_new_pad, v_new_pad, as_rows(ga))


CONV_TC = 256
CONV_HALO = 32
CONV_RB = 32
CONV_CB = 512
CONV_SHIFT_ROWS = 40


def _conv_finish(y, cb, g, b, gate):
    y = _layer_norm(y + cb, g, b)
    return y * _sigmoid(y) * gate


def _conv_prompt_kernel(u_ref, prev_ref, head_ref, gc_ref, w_ref, cb_ref, g_ref, b_ref, o_ref, win_ref, sh_ref, y_ref):
    i = pl.program_id(1)

    @pl.when(i == 0)
    def _():
        win_ref[0:CONV_HALO, :] = head_ref[...]

    @pl.when(i > 0)
    def _():
        win_ref[0:CONV_HALO, :] = prev_ref[...]

    win_ref[CONV_HALO:, :] = u_ref[...]
    n_sh = sh_ref.shape[1]
    for s in range(1, SUBLANES):
        for r in range(0, n_sh, CONV_SHIFT_ROWS):
            sh_ref[s - 1, r:r + CONV_SHIFT_ROWS, :] = win_ref[r + s:r + s + CONV_SHIFT_ROWS, :]
    off = CONV_HALO - (CONV_W - 1)
    n_ch = u_ref.shape[1]
    for r0 in range(0, CONV_TC, CONV_RB):
        for c0 in range(0, n_ch, CONV_CB):
            cs = slice(c0, c0 + CONV_CB)
            acc = jnp.zeros((CONV_RB, CONV_CB), F32)
            for t in range(CONV_W):
                s, base = (off + t) % SUBLANES, r0 + (off + t) // SUBLANES * SUBLANES
                rows = win_ref[base:base + CONV_RB, cs] if s == 0 else sh_ref[s - 1, base:base + CONV_RB, cs]
                acc = acc + rows * w_ref[t:t + 1, cs]
            y_ref[r0:r0 + CONV_RB, cs] = acc
    for r0 in range(0, CONV_TC, CONV_RB):
        rows = slice(r0, r0 + CONV_RB)
        gate = gc_ref[rows, :].astype(F32)
        o_ref[rows, :] = _conv_finish(y_ref[rows, :], cb_ref[...], g_ref[...], b_ref[...], gate).astype(o_ref.dtype)


def _conv_prompt(u, head, gc, conv_w, conv_b, g, b, n_batch, seq):
    c = u.shape[-1]
    u3 = u.reshape(n_batch, seq, c)
    gc3 = gc.reshape(n_batch, seq, c)
    ratio = CONV_TC // CONV_HALO
    vec = lambda: pl.BlockSpec((1, c), lambda bb, i: (0, 0))
    out = pl.pallas_call(
        _conv_prompt_kernel,
        out_shape=jax.ShapeDtypeStruct((n_batch, seq, c), BF16),
        grid=(n_batch, seq // CONV_TC),
        in_specs=[pl.BlockSpec((None, CONV_TC, c), lambda bb, i: (bb, i, 0)),
                  pl.BlockSpec((None, CONV_HALO, c), lambda bb, i: (bb, jnp.maximum(i * ratio - 1, 0), 0)),
                  pl.BlockSpec((CONV_HALO, c), lambda bb, i: (0, 0)),
                  pl.BlockSpec((None, CONV_TC, c), lambda bb, i: (bb, i, 0)),
                  pl.BlockSpec((CONV_W, c), lambda bb, i: (0, 0)),
                  vec(), vec(), vec()],
        out_specs=pl.BlockSpec((None, CONV_TC, c), lambda bb, i: (bb, i, 0)),
        scratch_shapes=[pltpu.VMEM((CONV_HALO + CONV_TC, c), F32),
                        pltpu.VMEM((SUBLANES - 1, CONV_HALO + CONV_TC - SUBLANES, c), F32),
                        pltpu.VMEM((CONV_TC, c), F32)],
        compiler_params=_params("parallel", "arbitrary"),
        name="conv_prompt",
    )(u3, u3, head, gc3, conv_w, conv_b.reshape(1, c), g.reshape(1, c), b.reshape(1, c))
    return out.reshape(n_batch * seq, c)


def _conv_sample_kernel(st_ref, u_ref, gc_ref, w_ref, cb_ref, g_ref, b_ref, o_ref):
    n_hist = st_ref.shape[0]
    n_new = u_ref.shape[0]
    for r in range(n_new):
        acc = jnp.zeros(u_ref.shape[1:], F32)
        for t in range(CONV_W):
            src = r + t
            row = st_ref[src] if src < n_hist else u_ref[src - n_hist]
            acc = acc + row * w_ref[t:t + 1, :]
        o_ref[r] = _conv_finish(acc, cb_ref[...], g_ref[...], b_ref[...], gc_ref[r])


def _conv_sample(state_t, u_t, gc_t, conv_w, conv_b, g, b):
    c = u_t.shape[-1]
    return pl.pallas_call(
        _conv_sample_kernel,
        out_shape=jax.ShapeDtypeStruct(u_t.shape, F32),
        compiler_params=pltpu.CompilerParams(vmem_limit_bytes=VMEM_LIMIT),
        name="conv_sample",
    )(state_t, u_t, gc_t, conv_w, conv_b.reshape(1, c), g.reshape(1, c), b.reshape(1, c))


def _out_kernel(x_ref, a_ref, c_ref, wa32_ref, wc32_ref, gi_ref, bi_ref, go_ref, bo_ref, o_ref, wa_ref, wc_ref,
                *, alpha):
    _cast_once([wa32_ref, wc32_ref], [wa_ref, wc_ref])
    h = _layer_norm(x_ref[...], gi_ref[...], bi_ref[...])
    z = jnp.dot(a_ref[...], wa_ref[...], preferred_element_type=F32)
    z = z + jnp.dot(c_ref[...], wc_ref[...], preferred_element_type=F32)
    o_ref[...] = _layer_norm(alpha * h + z, go_ref[...], bo_ref[...])


def _out_projection(x, a, c, w_out, gi, bi, go, bo, alpha, tm):
    m, d = x.shape
    n_a, n_c = a.shape[1], c.shape[1]
    assert n_a == n_c and w_out.shape == (n_a + n_c, d)
    row = lambda w: pl.BlockSpec((tm, w), lambda i: (i, 0))
    full = lambda arr: pl.BlockSpec(arr.shape, lambda i: (0, 0))
    w_blk = lambda j: pl.BlockSpec((n_a, d), lambda i: (j, 0), pipeline_mode=pl.Buffered(1))
    vecs = [v.reshape(1, d) for v in (gi, bi, go, bo)]
    return pl.pallas_call(
        functools.partial(_out_kernel, alpha=alpha),
        out_shape=jax.ShapeDtypeStruct((m, d), F32),
        grid=(m // tm,),
        in_specs=[row(d), row(n_a), row(n_c), w_blk(0), w_blk(1)] + [full(v) for v in vecs],
        out_specs=row(d),
        scratch_shapes=[pltpu.VMEM((n_a, d), BF16), pltpu.VMEM((n_c, d), BF16)],
        compiler_params=_params("arbitrary"),
        name="out_proj",
    )(x, a, c, w_out, w_out, *vecs)


def _split_w_in(w_t):
    assert N_IDX_HEADS * D_IDX == 2 * D_ATTN
    d_conv = (w_t.shape[0] - (3 * D_ATTN + N_IDX_HEADS * D_IDX + D_IDX + N_IDX_HEADS + D_ATTN)) // 3
    start = 3 * D_ATTN + N_IDX_HEADS * D_IDX + D_IDX + N_IDX_HEADS
    return dict(rest=w_t[start:].astype(BF16), d_conv=d_conv)


def kernel(x_prompt, x_sample, cache_k, cache_v, cache_kidx, state_conv, page_table, meta_tokens,
           ln_in_g, ln_in_b, w_in, ln_kidx_g, ln_kidx_b, conv_w, conv_b, ln_conv_g, ln_conv_b,
           w_out, ln_out_g, ln_out_b):
    n_batch, seq, d_model = x_prompt.shape
    db, n_new, _ = x_sample.shape
    depth = w_in.shape[0]
    assert depth == 1, "one mixer layer per step"
    assert seq >= CONV_W - 1
    n_pages = page_table.shape[1]
    past = n_pages * PAGE_SIZE
    n_sel_prompt = min(TOPK_MAX, seq // 4)
    n_sel_sample = min(TOPK_MAX, (past + n_new) // 4)
    alpha = (2.0 * depth) ** 0.25
    l = 0

    w_t = jnp.transpose(w_in[l])
    w = _split_w_in(w_t)
    d_conv = w["d_conv"]

    xp = x_prompt.reshape(n_batch * seq, d_model)
    xs = jnp.concatenate([meta_tokens.astype(F32), x_sample.reshape(db * n_new, d_model)], axis=0)
    n_small = xs.shape[0]
    pp = _in_projection(xp, ln_in_g, ln_in_b, w_t, w, ln_kidx_g[l], ln_kidx_b[l], 512)
    ps = _in_projection(xs, ln_in_g, ln_in_b, w_t, w, ln_kidx_g[l], ln_kidx_b[l], n_small)
    pm = {n: a[:N_META] for n, a in ps.items()}
    ps = {n: a[N_META:] for n, a in ps.items()}

    pad_rows = lambda a: jnp.pad(a, ((0, LANES - a.shape[0]), (0, 0)))
    meta = dict(ki=pad_rows(pm["ki16"]), k=pad_rows(pm["k16"]), vt=jnp.transpose(pad_rows(pm["v16"])))
    a_p = _prompt_attention(pp, meta, n_batch, seq, n_sel_prompt).reshape(n_batch * seq, D_ATTN)
    head = jnp.concatenate([jnp.zeros((CONV_HALO - N_META, d_conv), F32), pm["u"]], axis=0)
    c_p = _conv_prompt(pp["u"], head, pp["gc"], conv_w[l], conv_b[l], ln_conv_g[l], ln_conv_b[l], n_batch, seq)
    y_p = _out_projection(xp, a_p, c_p, w_out[l], ln_in_g, ln_in_b, ln_out_g[l], ln_out_b[l], alpha, 512)
    y_prompt = y_p.reshape(n_batch, seq, d_model)

    def with_meta(m_rows, p_rows):
        m_b = jnp.broadcast_to(m_rows[None], (n_batch,) + m_rows.shape)
        return jnp.concatenate([m_b, p_rows.reshape(n_batch, seq, -1)], axis=1)

    new_k_p = with_meta(pm["k32"], pp["k32"]).reshape(1, n_batch, N_META + seq, N_HEADS, HEAD_DIM)
    new_v_p = with_meta(pm["v32"], pp["v32"]).reshape(1, n_batch, N_META + seq, N_HEADS, HEAD_DIM)
    new_ki_p = with_meta(pm["ki32"], pp["ki32"])[None]
    new_conv_p = pp["u"].reshape(n_batch, seq, d_conv)[:, -(CONV_W - 1):][None]

    qi_rows = ps["qi"].reshape(db, n_new * N_IDX_HEADS, D_IDX)
    wcol = jnp.broadcast_to(ps["wi"][:, :N_IDX_HEADS].reshape(db, n_new * N_IDX_HEADS, 1),
                            (db, n_new * N_IDX_HEADS, LANES))
    pad_new = lambda a: jnp.pad(a.reshape(db, n_new, a.shape[-1]), ((0, 0), (0, LANES - n_new), (0, 0)))
    sc_past, sc_new = _sample_scores(page_table, qi_rows, wcol, pad_new(ps["ki16"]), cache_kidx, l)
    thr, j_max = _sample_threshold(sc_past, sc_new, n_sel_sample)
    key_rows = lambda c: c.reshape(-1, N_HEADS, HEAD_DIM)
    n_pool = cache_k.shape[1]
    k_sel, v_sel, cnt = _sample_gather(sc_past, thr, j_max, page_table + l * n_pool,
                                       key_rows(cache_k), key_rows(cache_v), n_sel_sample)
    q4 = ps["q"].reshape(db * n_new, N_HEADS, HEAD_DIM)
    q8 = jnp.einsum("rhd,hg->rhgd", q4, jnp.eye(N_HEADS, dtype=BF16)).reshape(db * n_new, N_HEADS, D_ATTN)
    a_s = _sample_attention(cnt.reshape(db * n_new, SC_LANES)[:, 0], q8, k_sel, v_sel, sc_new, thr, j_max,
                            pad_new(ps["k16"]), pad_new(ps["v16"]),
                            ps["ga"].astype(F32), n_new, past, n_sel_sample)
    a_s = a_s.reshape(db * n_new, D_ATTN).astype(BF16)

    to_t = lambda a: jnp.transpose(a.reshape(db, -1, d_conv), (1, 0, 2))
    c_t = _conv_sample(to_t(state_conv[l].astype(F32)), to_t(ps["u"]), to_t(ps["gc"].astype(F32)),
                       conv_w[l], conv_b[l], ln_conv_g[l], ln_conv_b[l])
    c_s = jnp.transpose(c_t, (1, 0, 2)).reshape(db * n_new, d_conv).astype(BF16)
    y_s = _out_projection(x_sample.reshape(db * n_new, d_model), a_s, c_s, w_out[l],
                          ln_in_g, ln_in_b, ln_out_g[l], ln_out_b[l], alpha, db * n_new)
    y_sample = y_s.reshape(db, n_new, d_model)

    new_k_s = ps["k32"].reshape(1, db, n_new, N_HEADS, HEAD_DIM)
    new_v_s = ps["v32"].reshape(1, db, n_new, N_HEADS, HEAD_DIM)
    new_ki_s = ps["ki32"].reshape(1, db, n_new, D_IDX)
    u_ext_s = jnp.concatenate([state_conv[l].astype(F32), ps["u"].reshape(db, n_new, d_conv)], axis=1)
    new_conv_s = u_ext_s[:, -(CONV_W - 1):][None]

    return (y_prompt, y_sample, new_k_p, new_v_p, new_ki_p, new_conv_p,
            new_k_s, new_v_s, new_ki_s, new_conv_s)
```

```python
import functools

import jax
import jax.numpy as jnp
from jax import lax
from jax.experimental import pallas as pl
from jax.experimental.pallas import tpu as pltpu
from jax.experimental.pallas import tpu_sc as plsc

N_META = 16
N_HEADS = 8
HEAD_DIM = 128
D_ATTN = N_HEADS * HEAD_DIM
N_IDX_HEADS = 16
D_IDX = 128
TOPK_MAX = 256
CONV_W = 31
PAGE_SIZE = 128
LN_EPS = 1e-5
ATTN_SCALE = HEAD_DIM ** -0.5
IDX_SCALE = (N_IDX_HEADS * D_IDX) ** -0.5

LANES = 128
SUBLANES = 8
NEG_BIG = -1e30
F32_MAX = 3.4028235e38
INT_MAX = 2 ** 31 - 1
SEARCH_CAP = 400
VMEM_LIMIT = 56 * 1024 * 1024

F32 = jnp.float32
BF16 = jnp.bfloat16
I32 = jnp.int32


def _params(*sem):
    return pltpu.CompilerParams(dimension_semantics=sem, vmem_limit_bytes=VMEM_LIMIT)


def _layer_norm(x, g, b):
    mu = jnp.mean(x, axis=-1, keepdims=True)
    xc = x - mu
    var = jnp.mean(xc * xc, axis=-1, keepdims=True)
    return xc * lax.rsqrt(var + LN_EPS) * g + b


def _sigmoid(z):
    return 1.0 / (1.0 + jnp.exp(-z))


def _dot_t(a, b):
    return lax.dot_general(a, b, (((1,), (1,)), ((), ())), preferred_element_type=F32)


def _fold_rows(x, reduce):
    return reduce(x.reshape(x.shape[0] // SUBLANES, SUBLANES, x.shape[1]), axis=0)


def _col_reduce(x, reduce):
    return reduce(_fold_rows(x, reduce), axis=0, keepdims=True)


def _topk_threshold(count, lo0, hi0, c0, active, k, n_total):
    one = jnp.ones(lo0.shape, I32)

    def cond(st):
        it, _, _, c_lo, stalled = st
        waiting = jnp.sum(jnp.where(active & (c_lo != k) & (stalled == 0), one, 0))
        return (it < SEARCH_CAP) & (waiting > 0)

    def step(st):
        it, lo, hi, c_lo, stalled = st
        mid = 0.5 * lo + 0.5 * hi
        stall = (mid <= lo) | (mid >= hi)
        c = count(lambda s, i: s >= mid)
        up = (c >= k) & jnp.logical_not(stall)
        down = (c < k) & jnp.logical_not(stall)
        return (it + 1, jnp.where(up, mid, lo), jnp.where(down, mid, hi), jnp.where(up, c, c_lo),
                jnp.where(stall, one, stalled))

    def body(st):
        return step(step(st))

    _, thr, _, c_lo, _ = lax.while_loop(cond, body, (jnp.int32(0), lo0, hi0, c0, jnp.zeros(lo0.shape, I32)))
    tied = active & (c_lo > k)

    def break_ties(_):
        need = k - count(lambda s, i: s > thr)

        def step(_, st):
            j_lo, j_hi = st
            j_mid = j_lo + (j_hi - j_lo) // 2
            ok = count(lambda s, i: (s == thr) & (i <= j_mid)) >= need
            return jnp.where(ok, j_lo, j_mid), jnp.where(ok, j_mid, j_hi)

        n_steps = (n_total + 1).bit_length()
        _, j_hi = lax.fori_loop(0, n_steps, step, (jnp.full(lo0.shape, -1, I32), jnp.full(lo0.shape, n_total, I32)))
        return jnp.where(tied, j_hi, INT_MAX)

    j_max = lax.cond(jnp.sum(jnp.where(tied, one, 0)) > 0, break_ties,
                     lambda _: jnp.full(lo0.shape, INT_MAX, I32), 0)
    thr = jnp.where(active, thr, -F32_MAX)
    return thr, j_max


def _above(x):
    return x + (jnp.abs(x) * 2.0 ** -20 + 1e-30)


def _silu(z):
    return z * _sigmoid(z)


def _cast_once(raw_refs, bf16_refs):
    @pl.when(pl.program_id(0) == 0)
    def _():
        for raw, dst in zip(raw_refs, bf16_refs):
            dst[...] = raw[...].astype(dst.dtype)


def _proj_ln_q_ga_kernel(x_ref, wq32_ref, wga_ref, g_ref, b_ref, h_ref, q_ref, ga_ref, wq_ref):
    _cast_once([wq32_ref], [wq_ref])
    h = _layer_norm(x_ref[...], g_ref[...], b_ref[...]).astype(BF16)
    h_ref[...] = h
    q_ref[...] = (_dot_t(h, wq_ref[...]) * ATTN_SCALE).astype(q_ref.dtype)
    ga_ref[...] = _silu(_dot_t(h, wga_ref[...])).astype(ga_ref.dtype)


def _proj_kv_kernel(h_ref, wk32_ref, wv32_ref, k32_ref, v32_ref, k16_ref, v16_ref, wk_ref, wv_ref):
    _cast_once([wk32_ref, wv32_ref], [wk_ref, wv_ref])
    h = h_ref[...]
    k = _dot_t(h, wk_ref[...])
    v = _dot_t(h, wv_ref[...])
    k32_ref[...] = k
    v32_ref[...] = v
    k16_ref[...] = k.astype(k16_ref.dtype)
    v16_ref[...] = v.astype(v16_ref.dtype)


def _proj_glu_gc_kernel(h_ref, wa_ref, wb_ref, wgc_ref, u_ref, gc_ref):
    h = h_ref[...]
    u_ref[...] = _dot_t(h, wa_ref[...]) * _sigmoid(_dot_t(h, wb_ref[...]))
    gc_ref[...] = _silu(_dot_t(h, wgc_ref[...])).astype(gc_ref.dtype)


def _proj_qi_kiwi_kernel(h_ref, wqa32_ref, wqb32_ref, wkiwi32_ref, g_ref, b_ref, qi_ref, ki32_ref, ki16_ref, wi_ref,
                         wqa_ref, wqb_ref, wkiwi_ref):
    _cast_once([wqa32_ref, wqb32_ref, wkiwi32_ref], [wqa_ref, wqb_ref, wkiwi_ref])
    h = h_ref[...]
    half = wqa_ref.shape[0]
    qi_ref[:, :half] = _dot_t(h, wqa_ref[...]).astype(qi_ref.dtype)
    qi_ref[:, half:] = _dot_t(h, wqb_ref[...]).astype(qi_ref.dtype)
    z = _dot_t(h, wkiwi_ref[...])
    ki = _layer_norm(z[:, :D_IDX], g_ref[...], b_ref[...])
    ki32_ref[...] = ki
    ki16_ref[...] = ki.astype(ki16_ref.dtype)
    lane = lax.broadcasted_iota(I32, wi_ref.shape, 1)
    wi_ref[...] = jnp.where(lane < N_IDX_HEADS, z[:, D_IDX:] * IDX_SCALE, 0.0)


def _proj(kernel, h, raw, ws, extra, outs, tm, name):
    m, d = h.shape
    w_t, cols = raw
    in_specs = [pl.BlockSpec((tm, d), lambda i: (i, 0))]
    in_specs += [pl.BlockSpec((width, d), lambda i, c=c: (c, 0), pipeline_mode=pl.Buffered(1))
                 for width, c in cols]
    in_specs += [pl.BlockSpec((rows, d), lambda i, c=c: (c, 0)) for _, rows, c in ws]
    in_specs += [pl.BlockSpec(e.shape, lambda i: (0, 0)) for e in extra]
    out_shape = [jax.ShapeDtypeStruct((m, n), dt) for n, dt in outs]
    out_specs = [pl.BlockSpec((tm, n), lambda i: (i, 0)) for n, _ in outs]
    return pl.pallas_call(
        kernel,
        out_shape=out_shape,
        grid=(m // tm,),
        in_specs=in_specs,
        out_specs=out_specs,
        scratch_shapes=[pltpu.VMEM((width, d), BF16) for width, _ in cols],
        compiler_params=_params("arbitrary"),
        name=name,
    )(h, *([w_t] * len(cols)), *[w for w, _, _ in ws], *extra)


def _in_projection(x, ln_g, ln_b, w_t, w, kidx_g, kidx_b, tm):
    d = x.shape[1]
    d_conv = w["d_conv"]
    rest = lambda i: (w["rest"], d_conv, i)
    assert D_ATTN == d_conv
    kiwi_w = 2 * LANES
    kiwi_col = 3 * D_ATTN + N_IDX_HEADS * D_IDX
    assert kiwi_col % kiwi_w == 0 and D_IDX + N_IDX_HEADS <= kiwi_w
    raw = lambda *cols: (w_t, list(cols))
    h, q, ga = _proj(_proj_ln_q_ga_kernel, x, raw((D_ATTN, 0)), [rest(0)],
                     [ln_g.reshape(1, d), ln_b.reshape(1, d)],
                     [(d, BF16), (D_ATTN, BF16), (D_ATTN, BF16)], tm, "proj_ln_q_gattn")
    k32, v32, k16, v16 = _proj(_proj_kv_kernel, h, raw((D_ATTN, 1), (D_ATTN, 2)), [], [],
                               [(D_ATTN, F32), (D_ATTN, F32), (D_ATTN, BF16), (D_ATTN, BF16)], tm, "proj_kv")
    qi, ki32, ki16, wi = _proj(_proj_qi_kiwi_kernel, h,
                               raw((D_ATTN, 3), (D_ATTN, 4), (kiwi_w, kiwi_col // kiwi_w)), [],
                               [kidx_g.reshape(1, D_IDX), kidx_b.reshape(1, D_IDX)],
                               [(N_IDX_HEADS * D_IDX, BF16), (D_IDX, F32), (D_IDX, BF16), (LANES, F32)],
                               tm, "proj_qi_kiwi")
    u, gc = _proj(_proj_glu_gc_kernel, h, raw(), [rest(1), rest(2), rest(3)], [],
                  [(d_conv, F32), (d_conv, BF16)], tm, "proj_glu_gconv")
    return dict(q=q, k32=k32, v32=v32, k16=k16, v16=v16, qi=qi, ki32=ki32, ki16=ki16, wi=wi, ga=ga, u=u, gc=gc)


TQ = 256
KC = 256


def _prompt_attn_kernel(qi_ref, wi_ref, q_ref, ga_ref, kip_ref, kp_ref, vtp_ref, kim_ref, km_ref, vtm_ref,
                        o_ref, sc_ref, scm_ref, wt_ref, s_ref, m_ref, l_ref, acc_ref, *, n_sel):
    qb = pl.program_id(1)
    n_chunks = (qb * TQ + TQ - 1) // KC + 1
    q_pos = qb * TQ + lax.broadcasted_iota(I32, (1, TQ), 1)
    row_kc = lax.broadcasted_iota(I32, (KC, TQ), 0)
    row_m = lax.broadcasted_iota(I32, (LANES, TQ), 0)

    wt_ref[...] = jnp.transpose(wi_ref[...])

    def idx_scores(ki):
        acc = jnp.zeros((ki.shape[0], TQ), F32)
        for h in range(N_IDX_HEADS):
            s = _dot_t(ki, qi_ref[:, h * D_IDX:(h + 1) * D_IDX])
            acc = acc + jnp.maximum(s, 0.0) * wt_ref[h:h + 1, :]
        return acc

    sm = idx_scores(kim_ref[...])
    meta_vis = row_m < N_META
    scm_ref[...] = jnp.where(meta_vis, sm, -jnp.inf)

    def score_chunk(j, carry):
        rmax, rmin = carry
        r0 = pl.multiple_of(j * KC, KC)
        s = idx_scores(kip_ref[pl.ds(r0, KC), :])
        vis = (j * KC + row_kc) <= q_pos
        s_vis = jnp.where(vis, s, -jnp.inf)
        sc_ref[j] = s_vis
        rmax = jnp.maximum(rmax, _fold_rows(s_vis, jnp.max))
        rmin = jnp.minimum(rmin, _fold_rows(jnp.where(vis, s, jnp.inf), jnp.min))
        return rmax, rmin

    rmax, rmin = lax.fori_loop(0, n_chunks, score_chunk,
                               (_fold_rows(jnp.where(meta_vis, sm, -jnp.inf), jnp.max),
                                _fold_rows(jnp.where(meta_vis, sm, jnp.inf), jnp.min)))

    def count(f):
        def ones(mask):
            return _fold_rows(jnp.where(mask, 1, 0), jnp.sum)

        def body(j, c):
            return c + ones(f(sc_ref[j], LANES + j * KC + row_kc))

        c = lax.fori_loop(0, n_chunks, body, ones(f(scm_ref[...], row_m)))
        return jnp.sum(c, axis=0, keepdims=True)

    n_vis = N_META + 1 + q_pos
    thr, j_max = _topk_threshold(count, jnp.min(rmin, axis=0, keepdims=True),
                                 _above(jnp.max(rmax, axis=0, keepdims=True)), n_vis,
                                 n_vis > n_sel, n_sel, LANES + sc_ref.shape[0] * KC)

    def bias_of(s, idx):
        keep = (s > thr) | ((s == thr) & (idx <= j_max))
        return jnp.where(keep, 0.0, NEG_BIG)

    m_ref[...] = jnp.full(m_ref.shape, NEG_BIG, F32)
    l_ref[...] = jnp.zeros(l_ref.shape, F32)
    acc_ref[...] = jnp.zeros(acc_ref.shape, F32)

    def attend(bias, k_blk, vt_blk):
        n = bias.shape[0]
        heads = [slice(h * HEAD_DIM, (h + 1) * HEAD_DIM) for h in range(N_HEADS)]
        m_new = []
        for h, cs in enumerate(heads):
            s = _dot_t(k_blk[:, cs], q_ref[:, cs]) + bias
            s_ref[h, 0:n, :] = s
            m_new.append(jnp.maximum(m_ref[h], _col_reduce(s, jnp.max)))
        for h, cs in enumerate(heads):
            alpha = jnp.exp(m_ref[h] - m_new[h])
            p = jnp.exp(s_ref[h, 0:n, :] - m_new[h])
            l_ref[h] = alpha * l_ref[h] + _col_reduce(p, jnp.sum)
            pv = jnp.dot(vt_blk[cs, :], p.astype(BF16), preferred_element_type=F32)
            acc_ref[cs, :] = alpha * acc_ref[cs, :] + pv
            m_ref[h] = m_new[h]

    attend(bias_of(scm_ref[...], row_m), km_ref[...], vtm_ref[...])

    def attn_chunk(j, carry):
        r0 = pl.multiple_of(j * KC, KC)
        attend(bias_of(sc_ref[j], LANES + j * KC + row_kc), kp_ref[pl.ds(r0, KC), :], vtp_ref[j])
        return carry

    lax.fori_loop(0, n_chunks, attn_chunk, 0)

    for h in range(N_HEADS):
        cs = slice(h * HEAD_DIM, (h + 1) * HEAD_DIM)
        out = jnp.transpose(acc_ref[cs, :] / l_ref[h])
        o_ref[:, cs] = (out * ga_ref[:, cs].astype(F32)).astype(o_ref.dtype)


def _prompt_attention(p, meta, n_batch, seq, n_sel):
    def rows(a):
        return a.reshape(n_batch, seq, a.shape[-1])

    qi, wi, q, ga, ki16, k16 = (rows(p[n]) for n in ("qi", "wi", "q", "ga", "ki16", "k16"))
    n_kc = seq // KC
    vt = jnp.transpose(p["v16"].reshape(n_batch, n_kc, KC, D_ATTN), (0, 1, 3, 2))
    blk = lambda w: pl.BlockSpec((None, TQ, w), lambda b, i: (b, i, 0))
    res = lambda w: pl.BlockSpec((None, seq, w), lambda b, i: (b, 0, 0), pipeline_mode=pl.Buffered(1))
    const = lambda a: pl.BlockSpec(a.shape, lambda b, i: (0, 0), pipeline_mode=pl.Buffered(1))
    return pl.pallas_call(
        functools.partial(_prompt_attn_kernel, n_sel=n_sel),
        out_shape=jax.ShapeDtypeStruct((n_batch, seq, D_ATTN), BF16),
        grid=(n_batch, seq // TQ),
        in_specs=[blk(N_IDX_HEADS * D_IDX), blk(LANES), blk(D_ATTN), blk(D_ATTN),
                  res(D_IDX), res(D_ATTN),
                  pl.BlockSpec((None, n_kc, D_ATTN, KC), lambda b, i: (b, 0, 0, 0), pipeline_mode=pl.Buffered(1)),
                  const(meta["ki"]), const(meta["k"]), const(meta["vt"])],
        out_specs=blk(D_ATTN),
        scratch_shapes=[pltpu.VMEM((n_kc, KC, TQ), F32),
                        pltpu.VMEM((LANES, TQ), F32),
                        pltpu.VMEM((LANES, TQ), F32),
                        pltpu.VMEM((N_HEADS, KC, TQ), F32),
                        pltpu.VMEM((N_HEADS, 1, TQ), F32),
                        pltpu.VMEM((N_HEADS, 1, TQ), F32),
                        pltpu.VMEM((D_ATTN, TQ), F32)],
        compiler_params=_params("parallel", "arbitrary"),
        name="prompt_attn",
    )(qi, wi, q, ga, ki16, k16, vt, meta["ki"], meta["k"], meta["vt"])


SCORE_PAGES = 32


def _idx_rows(qi, wcol, ki, n_q):
    a = jnp.maximum(_dot_t(qi, ki), 0.0) * wcol
    return jnp.sum(a.reshape(n_q, N_IDX_HEADS, ki.shape[0]), axis=1)


def _sample_score_kernel(pt_ref, qi_ref, wcol_ref, kin_ref, cache_ref, o_ref, on_ref, buf_ref, sem_ref, *, layer):
    steps = pl.num_programs(1)
    g = pl.program_id(0) * steps + pl.program_id(1)
    slot = lax.rem(g, 2)

    def page_copy(page, r, s):
        return pltpu.make_async_copy(cache_ref.at[layer, page], buf_ref.at[s, r], sem_ref.at[s])

    def fetch(g_next, s):
        b_next, j_next = g_next // steps, lax.rem(g_next, steps)

        def start(r, carry):
            page_copy(pt_ref[b_next, j_next * SCORE_PAGES + r], r, s).start()
            return carry

        lax.fori_loop(0, SCORE_PAGES, start, 0)

    @pl.when(g == 0)
    def _():
        fetch(g, slot)

    @pl.when(g + 1 < pl.num_programs(0) * steps)
    def _():
        fetch(g + 1, 1 - slot)

    def wait(r, carry):
        page_copy(0, r, slot).wait()
        return carry

    lax.fori_loop(0, SCORE_PAGES, wait, 0)

    n_q = o_ref.shape[0]
    qi = qi_ref[...]
    wcol = wcol_ref[...]
    for r in range(SCORE_PAGES):
        o_ref[:, r * PAGE_SIZE:(r + 1) * PAGE_SIZE] = _idx_rows(qi, wcol, buf_ref[slot, r].astype(BF16), n_q)

    @pl.when(pl.program_id(1) == steps - 1)
    def _():
        sc = _idx_rows(qi, wcol, kin_ref[...], n_q)
        lane = lax.broadcasted_iota(I32, (n_q, LANES), 1)
        row = lax.broadcasted_iota(I32, (n_q, LANES), 0)
        on_ref[...] = jnp.where(lane <= row, sc, -jnp.inf)


def _sample_scores(page_table, qi_rows, wcol, ki_new_pad, cache_kidx, layer):
    db, n_pages = page_table.shape
    n_rows = qi_rows.shape[1]
    n_q = n_rows // N_IDX_HEADS
    steps = n_pages // SCORE_PAGES
    per_b = lambda *shape: pl.BlockSpec((None,) + shape, lambda b, j, pt: (b, 0, 0))
    grid_spec = pltpu.PrefetchScalarGridSpec(
        num_scalar_prefetch=1,
        grid=(db, steps),
        in_specs=[per_b(n_rows, D_IDX), per_b(n_rows, LANES), per_b(LANES, D_IDX),
                  pl.BlockSpec(memory_space=pl.ANY)],
        out_specs=[pl.BlockSpec((None, n_q, SCORE_PAGES * PAGE_SIZE), lambda b, j, pt: (b, 0, j)),
                   per_b(n_q, LANES)],
        scratch_shapes=[pltpu.VMEM((2, SCORE_PAGES, PAGE_SIZE, D_IDX), F32), pltpu.SemaphoreType.DMA((2,))],
    )
    return pl.pallas_call(
        functools.partial(_sample_score_kernel, layer=layer),
        out_shape=[jax.ShapeDtypeStruct((db, n_q, n_pages * PAGE_SIZE), F32),
                   jax.ShapeDtypeStruct((db, n_q, LANES), F32)],
        grid_spec=grid_spec,
        compiler_params=_params("arbitrary", "arbitrary"),
        name="sample_scores",
    )(page_table, qi_rows, wcol, ki_new_pad, cache_kidx)


def _sample_thr_kernel(sp_ref, sn_ref, thr_ref, j_ref, *, n_sel, n_q):
    rows, past = sp_ref.shape
    lane = lax.broadcasted_iota(I32, (rows, LANES), 1)
    idx_p = lax.broadcasted_iota(I32, (rows, past), 1)
    sn = sn_ref[...]

    def count(f):
        c = jnp.sum(jnp.where(f(sp_ref[...], idx_p), 1, 0), axis=1, keepdims=True)
        return c + jnp.sum(jnp.where(f(sn, past + lane), 1, 0), axis=1, keepdims=True)

    new_vis = sn > -jnp.inf
    rmax = jnp.maximum(jnp.max(sp_ref[...], axis=1, keepdims=True), jnp.max(sn, axis=1, keepdims=True))
    rmin = jnp.minimum(jnp.min(sp_ref[...], axis=1, keepdims=True),
                       jnp.min(jnp.where(new_vis, sn, jnp.inf), axis=1, keepdims=True))
    n_vis = past + 1 + lax.rem(lax.broadcasted_iota(I32, (rows, 1), 0), n_q)
    thr, j_max = _topk_threshold(count, rmin, _above(rmax), n_vis, n_vis > n_sel, n_sel, past + LANES)
    thr_ref[...] = jnp.broadcast_to(thr, thr_ref.shape)
    j_ref[...] = jnp.broadcast_to(j_max, j_ref.shape)


def _sample_threshold(scores_past, scores_new, n_sel):
    db, n_q, past = scores_past.shape
    rows = db * n_q
    out = jax.ShapeDtypeStruct((rows, LANES), F32), jax.ShapeDtypeStruct((rows, LANES), I32)
    thr, j_max = pl.pallas_call(
        functools.partial(_sample_thr_kernel, n_sel=n_sel, n_q=n_q),
        out_shape=out,
        compiler_params=pltpu.CompilerParams(vmem_limit_bytes=VMEM_LIMIT),
        name="sample_threshold",
    )(scores_past.reshape(rows, past), scores_new.reshape(rows, LANES))
    return thr.reshape(db, n_q, LANES), j_max.reshape(db, n_q, LANES)


SC_LANES = 16
SC_WORKERS = 32
GATHER_ROWS = 32


def _sc_gather_kernel(sc_hbm, thr_hbm, jmax_hbm, pt_hbm, kc_hbm, vc_hbm, ksel_hbm, vsel_hbm, cnt_hbm,
                      sc_v, thr_v, jmax_v, pt_v, list_v, cnt_v, buf0, buf1, sem0, sem1,
                      *, n_q, past, n_pages, n_sel, rows_per_worker, n_cores):
    wid = lax.axis_index("s") * n_cores + lax.axis_index("c")
    lanes = lax.iota(I32, SC_LANES)
    zero = jnp.zeros((SC_LANES,), I32)
    bufs, sems = (buf0, buf1), (sem0, sem1)

    @pl.loop(0, rows_per_worker)
    def _(t):
        r = wid * rows_per_worker + t
        pltpu.sync_copy(pt_hbm.at[pl.ds((r // n_q) * n_pages, n_pages)], pt_v)
        pltpu.sync_copy(sc_hbm.at[pl.ds(r * past, past)], sc_v)
        pltpu.sync_copy(thr_hbm.at[pl.ds(r * LANES, SC_LANES)], thr_v)
        pltpu.sync_copy(jmax_hbm.at[pl.ds(r * LANES, SC_LANES)], jmax_v)
        thr = thr_v[...]
        j_max = jmax_v[...]
        for i in range(n_sel // SC_LANES):
            list_v[pl.ds(i * SC_LANES, SC_LANES)] = zero

        def body(i, cnt):
            base = i * SC_LANES
            s = sc_v[pl.ds(base, SC_LANES)]
            idx = base + lanes
            keep = (s > thr) | ((s == thr) & (idx <= j_max))
            page = plsc.load_gather(pt_v, [lax.shift_right_logical(idx, 7)])
            phys = page * PAGE_SIZE + (idx & (PAGE_SIZE - 1))
            pos = cnt + plsc.cumsum(jnp.where(keep, 1, 0)) - 1
            keep = keep & (pos < n_sel)
            plsc.store_scatter(list_v, [pos], phys, mask=keep)
            return cnt + plsc.all_reduce_population_count(keep)

        cnt = lax.fori_loop(0, past // SC_LANES, body, zero)
        cnt_v[...] = cnt
        pltpu.sync_copy(cnt_v, cnt_hbm.at[pl.ds(r * SC_LANES, SC_LANES)])

        n_pieces = n_sel // GATHER_ROWS
        jobs = [(kc_hbm, ksel_hbm, p) for p in range(n_pieces)] + [(vc_hbm, vsel_hbm, p) for p in range(n_pieces)]

        def gather(job, slot):
            src, _, p = job
            return pltpu.async_copy(src.at[list_v.at[pl.ds(p * GATHER_ROWS, GATHER_ROWS)]], bufs[slot], sems[slot])

        pending = gather(jobs[0], 0)
        for n, job in enumerate(jobs):
            pending.wait()
            if n + 1 < len(jobs):
                pending = gather(jobs[n + 1], (n + 1) % 2)
            _, dst, p = job
            pltpu.sync_copy(bufs[n % 2], dst.at[pl.ds(r * n_sel + p * GATHER_ROWS, GATHER_ROWS)])


def _sample_gather(scores_past, thr, j_max, page_table, cache_k_rows, cache_v_rows, n_sel):
    db, n_q, past = scores_past.shape
    rows = db * n_q
    n_pages = page_table.shape[1]
    info = plsc.get_sparse_core_info()
    assert info.num_lanes == SC_LANES and info.num_cores * info.num_subcores == SC_WORKERS
    assert rows % SC_WORKERS == 0 and n_sel % GATHER_ROWS == 0 and past % SC_LANES == 0
    mesh = plsc.VectorSubcoreMesh(core_axis_name="c", subcore_axis_name="s")
    sel = jax.ShapeDtypeStruct((rows * n_sel, N_HEADS, HEAD_DIM), F32)
    body = functools.partial(_sc_gather_kernel, n_q=n_q, past=past, n_pages=n_pages, n_sel=n_sel,
                             rows_per_worker=rows // SC_WORKERS, n_cores=info.num_cores)
    return pl.kernel(
        body,
        out_type=[sel, sel, jax.ShapeDtypeStruct((rows * SC_LANES,), I32)],
        mesh=mesh,
        scratch_types=[pltpu.VMEM((past,), F32), pltpu.VMEM((SC_LANES,), F32), pltpu.VMEM((SC_LANES,), I32),
                       pltpu.VMEM((n_pages,), I32), pltpu.VMEM((n_sel,), I32), pltpu.VMEM((SC_LANES,), I32),
                       pltpu.VMEM((GATHER_ROWS, N_HEADS, HEAD_DIM), F32),
                       pltpu.VMEM((GATHER_ROWS, N_HEADS, HEAD_DIM), F32),
                       pltpu.SemaphoreType.DMA, pltpu.SemaphoreType.DMA],
        compiler_params=pltpu.CompilerParams(needs_layout_passes=False),
        name="sample_gather",
    )(scores_past.reshape(-1), thr.reshape(-1), j_max.reshape(-1), page_table.reshape(-1),
      cache_k_rows, cache_v_rows)


def _sample_attn_kernel(cnt_ref, q_ref, k_ref, v_ref, sn_ref, thr_ref, jmax_ref, kn_ref, vn_ref, ga_ref, o_ref,
                        *, past, n_sel):
    r = pl.program_id(0)
    q8 = q_ref[...]

    def flat(rows_ref):
        heads = [rows_ref[pl.ds(h, n_sel, stride=N_HEADS), :] for h in range(N_HEADS)]
        return jnp.concatenate(heads, axis=1).astype(BF16)

    slot = lax.broadcasted_iota(I32, (N_HEADS, n_sel), 1)
    s_p = jnp.where(slot < cnt_ref[r], _dot_t(q8, flat(k_ref)), NEG_BIG)
    sn = sn_ref[...]
    idx_n = past + lax.broadcasted_iota(I32, sn.shape, 1)
    thr = thr_ref[:, 0:1]
    keep_n = (sn > thr) | ((sn == thr) & (idx_n <= jmax_ref[:, 0:1]))
    s_n = _dot_t(q8, kn_ref[...]) + jnp.where(keep_n, 0.0, NEG_BIG)

    m = jnp.maximum(jnp.max(s_p, axis=1, keepdims=True), jnp.max(s_n, axis=1, keepdims=True))
    p_p = jnp.exp(s_p - m)
    p_n = jnp.exp(s_n - m)
    den = jnp.sum(p_p, axis=1, keepdims=True) + jnp.sum(p_n, axis=1, keepdims=True)
    out8 = jnp.dot(p_p.astype(BF16), flat(v_ref), preferred_element_type=F32)
    out8 = (out8 + jnp.dot(p_n.astype(BF16), vn_ref[...], preferred_element_type=F32)) / den
    row = lax.broadcasted_iota(I32, out8.shape, 0)
    col_head = lax.broadcasted_iota(I32, out8.shape, 1) // HEAD_DIM
    o_ref[...] = jnp.sum(jnp.where(row == col_head, out8, 0.0), axis=0, keepdims=True) * ga_ref[...]


def _sample_attention(cnt, q8, k_sel, v_sel, scores_new, thr, j_max, k_new_pad, v_new_pad, ga, n_q, past, n_sel):
    rows = q8.shape[0]
    row3 = lambda *shape: pl.BlockSpec((None,) + shape, lambda r, c: (r, 0, 0))
    per_b = lambda *shape: pl.BlockSpec((None,) + shape, lambda r, c: (r // n_q, 0, 0))
    grid_spec = pltpu.PrefetchScalarGridSpec(
        num_scalar_prefetch=1,
        grid=(rows,),
        in_specs=[row3(N_HEADS, D_ATTN), row3(n_sel * N_HEADS, HEAD_DIM), row3(n_sel * N_HEADS, HEAD_DIM),
                  row3(1, LANES), row3(1, LANES), row3(1, LANES),
                  per_b(LANES, D_ATTN), per_b(LANES, D_ATTN), row3(1, D_ATTN)],
        out_specs=row3(1, D_ATTN),
    )
    as_rows = lambda a: a.reshape(rows, 1, a.shape[-1])
    return pl.pallas_call(
        functools.partial(_sample_attn_kernel, past=past, n_sel=n_sel),
        out_shape=jax.ShapeDtypeStruct((rows, 1, D_ATTN), F32),
        grid_spec=grid_spec,
        compiler_params=_params("parallel"),
        name="sample_attn",
    )(cnt, q8, k_sel.reshape(rows, n_sel * N_HEADS, HEAD_DIM), v_sel.reshape(rows, n_sel * N_HEADS, HEAD_DIM),
      as_rows(scores_new), as_rows(thr), as_rows(j_max), k_new_pad, v_new_pad, as_rows(ga))


CONV_TC = 256
CONV_HALO = 32
CONV_RB = 32
CONV_CB = 512
CONV_SHIFT_ROWS = 40


def _conv_finish(y, cb, g, b, gate):
    y = _layer_norm(y + cb, g, b)
    return y * _sigmoid(y) * gate


def _conv_prompt_kernel(u_ref, prev_ref, head_ref, gc_ref, w_ref, cb_ref, g_ref, b_ref, o_ref, win_ref, sh_ref, y_ref):
    i = pl.program_id(1)

    @pl.when(i == 0)
    def _():
        win_ref[0:CONV_HALO, :] = head_ref[...]

    @pl.when(i > 0)
    def _():
        win_ref[0:CONV_HALO, :] = prev_ref[...]

    win_ref[CONV_HALO:, :] = u_ref[...]
    n_sh = sh_ref.shape[1]
    for s in range(1, SUBLANES):
        for r in range(0, n_sh, CONV_SHIFT_ROWS):
            sh_ref[s - 1, r:r + CONV_SHIFT_ROWS, :] = win_ref[r + s:r + s + CONV_SHIFT_ROWS, :]
    off = CONV_HALO - (CONV_W - 1)
    n_ch = u_ref.shape[1]
    for r0 in range(0, CONV_TC, CONV_RB):
        for c0 in range(0, n_ch, CONV_CB):
            cs = slice(c0, c0 + CONV_CB)
            acc = jnp.zeros((CONV_RB, CONV_CB), F32)
            for t in range(CONV_W):
                s, base = (off + t) % SUBLANES, r0 + (off + t) // SUBLANES * SUBLANES
                rows = win_ref[base:base + CONV_RB, cs] if s == 0 else sh_ref[s - 1, base:base + CONV_RB, cs]
                acc = acc + rows * w_ref[t:t + 1, cs]
            y_ref[r0:r0 + CONV_RB, cs] = acc
    for r0 in range(0, CONV_TC, CONV_RB):
        rows = slice(r0, r0 + CONV_RB)
        gate = gc_ref[rows, :].astype(F32)
        o_ref[rows, :] = _conv_finish(y_ref[rows, :], cb_ref[...], g_ref[...], b_ref[...], gate).astype(o_ref.dtype)


def _conv_prompt(u, head, gc, conv_w, conv_b, g, b, n_batch, seq):
    c = u.shape[-1]
    u3 = u.reshape(n_batch, seq, c)
    gc3 = gc.reshape(n_batch, seq, c)
    ratio = CONV_TC // CONV_HALO
    vec = lambda: pl.BlockSpec((1, c), lambda bb, i: (0, 0))
    out = pl.pallas_call(
        _conv_prompt_kernel,
        out_shape=jax.ShapeDtypeStruct((n_batch, seq, c), BF16),
        grid=(n_batch, seq // CONV_TC),
        in_specs=[pl.BlockSpec((None, CONV_TC, c), lambda bb, i: (bb, i, 0)),
                  pl.BlockSpec((None, CONV_HALO, c), lambda bb, i: (bb, jnp.maximum(i * ratio - 1, 0), 0)),
                  pl.BlockSpec((CONV_HALO, c), lambda bb, i: (0, 0)),
                  pl.BlockSpec((None, CONV_TC, c), lambda bb, i: (bb, i, 0)),
                  pl.BlockSpec((CONV_W, c), lambda bb, i: (0, 0)),
                  vec(), vec(), vec()],
        out_specs=pl.BlockSpec((None, CONV_TC, c), lambda bb, i: (bb, i, 0)),
        scratch_shapes=[pltpu.VMEM((CONV_HALO + CONV_TC, c), F32),
                        pltpu.VMEM((SUBLANES - 1, CONV_HALO + CONV_TC - SUBLANES, c), F32),
                        pltpu.VMEM((CONV_TC, c), F32)],
        compiler_params=_params("parallel", "arbitrary"),
        name="conv_prompt",
    )(u3, u3, head, gc3, conv_w, conv_b.reshape(1, c), g.reshape(1, c), b.reshape(1, c))
    return out.reshape(n_batch * seq, c)


def _conv_sample_kernel(st_ref, u_ref, gc_ref, w_ref, cb_ref, g_ref, b_ref, o_ref):
    n_hist = st_ref.shape[0]
    n_new = u_ref.shape[0]
    for r in range(n_new):
        acc = jnp.zeros(u_ref.shape[1:], F32)
        for t in range(CONV_W):
            src = r + t
            row = st_ref[src] if src < n_hist else u_ref[src - n_hist]
            acc = acc + row * w_ref[t:t + 1, :]
        o_ref[r] = _conv_finish(acc, cb_ref[...], g_ref[...], b_ref[...], gc_ref[r])


def _conv_sample(state_t, u_t, gc_t, conv_w, conv_b, g, b):
    c = u_t.shape[-1]
    return pl.pallas_call(
        _conv_sample_kernel,
        out_shape=jax.ShapeDtypeStruct(u_t.shape, F32),
        compiler_params=pltpu.CompilerParams(vmem_limit_bytes=VMEM_LIMIT),
        name="conv_sample",
    )(state_t, u_t, gc_t, conv_w, conv_b.reshape(1, c), g.reshape(1, c), b.reshape(1, c))


def _out_kernel(x_ref, a_ref, c_ref, wa32_ref, wc32_ref, gi_ref, bi_ref, go_ref, bo_ref, o_ref, wa_ref, wc_ref,
                *, alpha):
    _cast_once([wa32_ref, wc32_ref], [wa_ref, wc_ref])
    h = _layer_norm(x_ref[...], gi_ref[...], bi_ref[...])
    z = jnp.dot(a_ref[...], wa_ref[...], preferred_element_type=F32)
    z = z + jnp.dot(c_ref[...], wc_ref[...], preferred_element_type=F32)
    o_ref[...] = _layer_norm(alpha * h + z, go_ref[...], bo_ref[...])


def _out_projection(x, a, c, w_out, gi, bi, go, bo, alpha, tm):
    m, d = x.shape
    n_a, n_c = a.shape[1], c.shape[1]
    assert n_a == n_c and w_out.shape == (n_a + n_c, d)
    row = lambda w: pl.BlockSpec((tm, w), lambda i: (i, 0))
    full = lambda arr: pl.BlockSpec(arr.shape, lambda i: (0, 0))
    w_blk = lambda j: pl.BlockSpec((n_a, d), lambda i: (j, 0), pipeline_mode=pl.Buffered(1))
    vecs = [v.reshape(1, d) for v in (gi, bi, go, bo)]
    return pl.pallas_call(
        functools.partial(_out_kernel, alpha=alpha),
        out_shape=jax.ShapeDtypeStruct((m, d), F32),
        grid=(m // tm,),
        in_specs=[row(d), row(n_a), row(n_c), w_blk(0), w_blk(1)] + [full(v) for v in vecs],
        out_specs=row(d),
        scratch_shapes=[pltpu.VMEM((n_a, d), BF16), pltpu.VMEM((n_c, d), BF16)],
        compiler_params=_params("arbitrary"),
        name="out_proj",
    )(x, a, c, w_out, w_out, *vecs)


def _split_w_in(w_t):
    assert N_IDX_HEADS * D_IDX == 2 * D_ATTN
    d_conv = (w_t.shape[0] - (3 * D_ATTN + N_IDX_HEADS * D_IDX + D_IDX + N_IDX_HEADS + D_ATTN)) // 3
    start = 3 * D_ATTN + N_IDX_HEADS * D_IDX + D_IDX + N_IDX_HEADS
    return dict(rest=w_t[start:].astype(BF16), d_conv=d_conv)


def kernel(x_prompt, x_sample, cache_k, cache_v, cache_kidx, state_conv, page_table, meta_tokens,
           ln_in_g, ln_in_b, w_in, ln_kidx_g, ln_kidx_b, conv_w, conv_b, ln_conv_g, ln_conv_b,
           w_out, ln_out_g, ln_out_b):
    n_batch, seq, d_model = x_prompt.shape
    db, n_new, _ = x_sample.shape
    depth = w_in.shape[0]
    assert depth == 1, "one mixer layer per step"
    assert seq >= CONV_W - 1
    n_pages = page_table.shape[1]
    past = n_pages * PAGE_SIZE
    n_sel_prompt = min(TOPK_MAX, seq // 4)
    n_sel_sample = min(TOPK_MAX, (past + n_new) // 4)
    alpha = (2.0 * depth) ** 0.25
    l = 0

    w_t = jnp.transpose(w_in[l])
    w = _split_w_in(w_t)
    d_conv = w["d_conv"]

    xp = x_prompt.reshape(n_batch * seq, d_model)
    xs = jnp.concatenate([meta_tokens.astype(F32), x_sample.reshape(db * n_new, d_model)], axis=0)
    n_small = xs.shape[0]
    pp = _in_projection(xp, ln_in_g, ln_in_b, w_t, w, ln_kidx_g[l], ln_kidx_b[l], 512)
    ps = _in_projection(xs, ln_in_g, ln_in_b, w_t, w, ln_kidx_g[l], ln_kidx_b[l], n_small)
    pm = {n: a[:N_META] for n, a in ps.items()}
    ps = {n: a[N_META:] for n, a in ps.items()}

    pad_rows = lambda a: jnp.pad(a, ((0, LANES - a.shape[0]), (0, 0)))
    meta = dict(ki=pad_rows(pm["ki16"]), k=pad_rows(pm["k16"]), vt=jnp.transpose(pad_rows(pm["v16"])))
    a_p = _prompt_attention(pp, meta, n_batch, seq, n_sel_prompt).reshape(n_batch * seq, D_ATTN)
    head = jnp.concatenate([jnp.zeros((CONV_HALO - N_META, d_conv), F32), pm["u"]], axis=0)
    c_p = _conv_prompt(pp["u"], head, pp["gc"], conv_w[l], conv_b[l], ln_conv_g[l], ln_conv_b[l], n_batch, seq)
    y_p = _out_projection(xp, a_p, c_p, w_out[l], ln_in_g, ln_in_b, ln_out_g[l], ln_out_b[l], alpha, 512)
    y_prompt = y_p.reshape(n_batch, seq, d_model)

    def with_meta(m_rows, p_rows):
        m_b = jnp.broadcast_to(m_rows[None], (n_batch,) + m_rows.shape)
        return jnp.concatenate([m_b, p_rows.reshape(n_batch, seq, -1)], axis=1)

    new_k_p = with_meta(pm["k32"], pp["k32"]).reshape(1, n_batch, N_META + seq, N_HEADS, HEAD_DIM)
    new_v_p = with_meta(pm["v32"], pp["v32"]).reshape(1, n_batch, N_META + seq, N_HEADS, HEAD_DIM)
    new_ki_p = with_meta(pm["ki32"], pp["ki32"])[None]
    new_conv_p = pp["u"].reshape(n_batch, seq, d_conv)[:, -(CONV_W - 1):][None]

    qi_rows = ps["qi"].reshape(db, n_new * N_IDX_HEADS, D_IDX)
    wcol = jnp.broadcast_to(ps["wi"][:, :N_IDX_HEADS].reshape(db, n_new * N_IDX_HEADS, 1),
                            (db, n_new * N_IDX_HEADS, LANES))
    pad_new = lambda a: jnp.pad(a.reshape(db, n_new, a.shape[-1]), ((0, 0), (0, LANES - n_new), (0, 0)))
    sc_past, sc_new = _sample_scores(page_table, qi_rows, wcol, pad_new(ps["ki16"]), cache_kidx, l)
    thr, j_max = _sample_threshold(sc_past, sc_new, n_sel_sample)
    key_rows = lambda c: c.reshape(-1, N_HEADS, HEAD_DIM)
    n_pool = cache_k.shape[1]
    k_sel, v_sel, cnt = _sample_gather(sc_past, thr, j_max, page_table + l * n_pool,
                                       key_rows(cache_k), key_rows(cache_v), n_sel_sample)
    q4 = ps["q"].reshape(db * n_new, N_HEADS, HEAD_DIM)
    q8 = jnp.einsum("rhd,hg->rhgd", q4, jnp.eye(N_HEADS, dtype=BF16)).reshape(db * n_new, N_HEADS, D_ATTN)
    a_s = _sample_attention(cnt.reshape(db * n_new, SC_LANES)[:, 0], q8, k_sel, v_sel, sc_new, thr, j_max,
                            pad_new(ps["k16"]), pad_new(ps["v16"]),
                            ps["ga"].astype(F32), n_new, past, n_sel_sample)
    a_s = a_s.reshape(db * n_new, D_ATTN).astype(BF16)

    to_t = lambda a: jnp.transpose(a.reshape(db, -1, d_conv), (1, 0, 2))
    c_t = _conv_sample(to_t(state_conv[l].astype(F32)), to_t(ps["u"]), to_t(ps["gc"].astype(F32)),
                       conv_w[l], conv_b[l], ln_conv_g[l], ln_conv_b[l])
    c_s = jnp.transpose(c_t, (1, 0, 2)).reshape(db * n_new, d_conv).astype(BF16)
    y_s = _out_projection(x_sample.reshape(db * n_new, d_model), a_s, c_s, w_out[l],
                          ln_in_g, ln_in_b, ln_out_g[l], ln_out_b[l], alpha, db * n_new)
    y_sample = y_s.reshape(db, n_new, d_model)

    new_k_s = ps["k32"].reshape(1, db, n_new, N_HEADS, HEAD_DIM)
    new_v_s = ps["v32"].reshape(1, db, n_new, N_HEADS, HEAD_DIM)
    new_ki_s = ps["ki32"].reshape(1, db, n_new, D_IDX)
    u_ext_s = jnp.concatenate([state_conv[l].astype(F32), ps["u"].reshape(db, n_new, d_conv)], axis=1)
    new_conv_s = u_ext_s[:, -(CONV_W - 1):][None]

    return (y_prompt, y_sample, new_k_p, new_v_p, new_ki_p, new_conv_p,
            new_k_s, new_v_s, new_ki_s, new_conv_s)
```

```python
import functools

import jax
import jax.numpy as jnp
from jax import lax
from jax.experimental import pallas as pl
from jax.experimental.pallas import tpu as pltpu
from jax.experimental.pallas import tpu_sc as plsc

N_META = 16
N_HEADS = 8
HEAD_DIM = 128
D_ATTN = N_HEADS * HEAD_DIM
N_IDX_HEADS = 16
D_IDX = 128
TOPK_MAX = 256
CONV_W = 31
PAGE_SIZE = 128
LN_EPS = 1e-5
ATTN_SCALE = HEAD_DIM ** -0.5
IDX_SCALE = (N_IDX_HEADS * D_IDX) ** -0.5

LANES = 128
SUBLANES = 8
NEG_BIG = -1e30
F32_MAX = 3.4028235e38
INT_MAX = 2 ** 31 - 1
SEARCH_CAP = 400
VMEM_LIMIT = 56 * 1024 * 1024

F32 = jnp.float32
BF16 = jnp.bfloat16
I32 = jnp.int32


def _params(*sem):
    return pltpu.CompilerParams(dimension_semantics=sem, vmem_limit_bytes=VMEM_LIMIT)


def _layer_norm(x, g, b):
    mu = jnp.mean(x, axis=-1, keepdims=True)
    xc = x - mu
    var = jnp.mean(xc * xc, axis=-1, keepdims=True)
    return xc * lax.rsqrt(var + LN_EPS) * g + b


def _sigmoid(z):
    return 1.0 / (1.0 + jnp.exp(-z))


def _dot_t(a, b):
    return lax.dot_general(a, b, (((1,), (1,)), ((), ())), preferred_element_type=F32)


def _fold_rows(x, reduce):
    return reduce(x.reshape(x.shape[0] // SUBLANES, SUBLANES, x.shape[1]), axis=0)


def _col_reduce(x, reduce):
    return reduce(_fold_rows(x, reduce), axis=0, keepdims=True)


def _topk_threshold(count, lo0, hi0, c0, active, k, n_total):
    one = jnp.ones(lo0.shape, I32)

    def cond(st):
        it, _, _, c_lo, stalled = st
        waiting = jnp.sum(jnp.where(active & (c_lo != k) & (stalled == 0), one, 0))
        return (it < SEARCH_CAP) & (waiting > 0)

    def step(st):
        it, lo, hi, c_lo, stalled = st
        mid = 0.5 * lo + 0.5 * hi
        stall = (mid <= lo) | (mid >= hi)
        c = count(lambda s, i: s >= mid)
        up = (c >= k) & jnp.logical_not(stall)
        down = (c < k) & jnp.logical_not(stall)
        return (it + 1, jnp.where(up, mid, lo), jnp.where(down, mid, hi), jnp.where(up, c, c_lo),
                jnp.where(stall, one, stalled))

    def body(st):
        return step(step(st))

    _, thr, _, c_lo, _ = lax.while_loop(cond, body, (jnp.int32(0), lo0, hi0, c0, jnp.zeros(lo0.shape, I32)))
    tied = active & (c_lo > k)

    def break_ties(_):
        need = k - count(lambda s, i: s > thr)

        def step(_, st):
            j_lo, j_hi = st
            j_mid = j_lo + (j_hi - j_lo) // 2
            ok = count(lambda s, i: (s == thr) & (i <= j_mid)) >= need
            return jnp.where(ok, j_lo, j_mid), jnp.where(ok, j_mid, j_hi)

        n_steps = (n_total + 1).bit_length()
        _, j_hi = lax.fori_loop(0, n_steps, step, (jnp.full(lo0.shape, -1, I32), jnp.full(lo0.shape, n_total, I32)))
        return jnp.where(tied, j_hi, INT_MAX)

    j_max = lax.cond(jnp.sum(jnp.where(tied, one, 0)) > 0, break_ties,
                     lambda _: jnp.full(lo0.shape, INT_MAX, I32), 0)
    thr = jnp.where(active, thr, -F32_MAX)
    return thr, j_max


def _above(x):
    return x + (jnp.abs(x) * 2.0 ** -20 + 1e-30)


def _silu(z):
    return z * _sigmoid(z)


def _cast_once(raw_refs, bf16_refs):
    @pl.when(pl.program_id(0) == 0)
    def _():
        for raw, dst in zip(raw_refs, bf16_refs):
            dst[...] = raw[...].astype(dst.dtype)


def _proj_ln_q_ga_kernel(x_ref, wq32_ref, wga_ref, g_ref, b_ref, h_ref, q_ref, ga_ref, wq_ref):
    _cast_once([wq32_ref], [wq_ref])
    h = _layer_norm(x_ref[...], g_ref[...], b_ref[...]).astype(BF16)
    h_ref[...] = h
    q_ref[...] = (_dot_t(h, wq_ref[...]) * ATTN_SCALE).astype(q_ref.dtype)
    ga_ref[...] = _silu(_dot_t(h, wga_ref[...])).astype(ga_ref.dtype)


def _proj_kv_kernel(h_ref, wk32_ref, wv32_ref, k32_ref, v32_ref, k16_ref, v16_ref, wk_ref, wv_ref):
    _cast_once([wk32_ref, wv32_ref], [wk_ref, wv_ref])
    h = h_ref[...]
    k = _dot_t(h, wk_ref[...])
    v = _dot_t(h, wv_ref[...])
    k32_ref[...] = k
    v32_ref[...] = v
    k16_ref[...] = k.astype(k16_ref.dtype)
    v16_ref[...] = v.astype(v16_ref.dtype)


def _proj_glu_gc_kernel(h_ref, wa_ref, wb_ref, wgc_ref, u_ref, gc_ref):
    h = h_ref[...]
    u_ref[...] = _dot_t(h, wa_ref[...]) * _sigmoid(_dot_t(h, wb_ref[...]))
    gc_ref[...] = _silu(_dot_t(h, wgc_ref[...])).astype(gc_ref.dtype)


def _proj_qi_kiwi_kernel(h_ref, wqa32_ref, wqb32_ref, wkiwi32_ref, g_ref, b_ref, qi_ref, ki32_ref, ki16_ref, wi_ref,
                         wqa_ref, wqb_ref, wkiwi_ref):
    _cast_once([wqa32_ref, wqb32_ref, wkiwi32_ref], [wqa_ref, wqb_ref, wkiwi_ref])
    h = h_ref[...]
    half = wqa_ref.shape[0]
    qi_ref[:, :half] = _dot_t(h, wqa_ref[...]).astype(qi_ref.dtype)
    qi_ref[:, half:] = _dot_t(h, wqb_ref[...]).astype(qi_ref.dtype)
    z = _dot_t(h, wkiwi_ref[...])
    ki = _layer_norm(z[:, :D_IDX], g_ref[...], b_ref[...])
    ki32_ref[...] = ki
    ki16_ref[...] = ki.astype(ki16_ref.dtype)
    lane = lax.broadcasted_iota(I32, wi_ref.shape, 1)
    wi_ref[...] = jnp.where(lane < N_IDX_HEADS, z[:, D_IDX:] * IDX_SCALE, 0.0)


def _proj(kernel, h, raw, ws, extra, outs, tm, name):
    m, d = h.shape
    w_t, cols = raw
    in_specs = [pl.BlockSpec((tm, d), lambda i: (i, 0))]
    in_specs += [pl.BlockSpec((width, d), lambda i, c=c: (c, 0), pipeline_mode=pl.Buffered(1))
                 for width, c in cols]
    in_specs += [pl.BlockSpec((rows, d), lambda i, c=c: (c, 0)) for _, rows, c in ws]
    in_specs += [pl.BlockSpec(e.shape, lambda i: (0, 0)) for e in extra]
    out_shape = [jax.ShapeDtypeStruct((m, n), dt) for n, dt in outs]
    out_specs = [pl.BlockSpec((tm, n), lambda i: (i, 0)) for n, _ in outs]
    return pl.pallas_call(
        kernel,
        out_shape=out_shape,
        grid=(m // tm,),
        in_specs=in_specs,
        out_specs=out_specs,
        scratch_shapes=[pltpu.VMEM((width, d), BF16) for width, _ in cols],
        compiler_params=_params("arbitrary"),
        name=name,
    )(h, *([w_t] * len(cols)), *[w for w, _, _ in ws], *extra)


def _in_projection(x, ln_g, ln_b, w_t, w, kidx_g, kidx_b, tm):
    d = x.shape[1]
    d_conv = w["d_conv"]
    rest = lambda i: (w["rest"], d_conv, i)
    assert D_ATTN == d_conv
    kiwi_w = 2 * LANES
    kiwi_col = 3 * D_ATTN + N_IDX_HEADS * D_IDX
    assert kiwi_col % kiwi_w == 0 and D_IDX + N_IDX_HEADS <= kiwi_w
    raw = lambda *cols: (w_t, list(cols))
    h, q, ga = _proj(_proj_ln_q_ga_kernel, x, raw((D_ATTN, 0)), [rest(0)],
                     [ln_g.reshape(1, d), ln_b.reshape(1, d)],
                     [(d, BF16), (D_ATTN, BF16), (D_ATTN, BF16)], tm, "proj_ln_q_gattn")
    k32, v32, k16, v16 = _proj(_proj_kv_kernel, h, raw((D_ATTN, 1), (D_ATTN, 2)), [], [],
                               [(D_ATTN, F32), (D_ATTN, F32), (D_ATTN, BF16), (D_ATTN, BF16)], tm, "proj_kv")
    qi, ki32, ki16, wi = _proj(_proj_qi_kiwi_kernel, h,
                               raw((D_ATTN, 3), (D_ATTN, 4), (kiwi_w, kiwi_col // kiwi_w)), [],
                               [kidx_g.reshape(1, D_IDX), kidx_b.reshape(1, D_IDX)],
                               [(N_IDX_HEADS * D_IDX, BF16), (D_IDX, F32), (D_IDX, BF16), (LANES, F32)],
                               tm, "proj_qi_kiwi")
    u, gc = _proj(_proj_glu_gc_kernel, h, raw(), [rest(1), rest(2), rest(3)], [],
                  [(d_conv, F32), (d_conv, BF16)], tm, "proj_glu_gconv")
    return dict(q=q, k32=k32, v32=v32, k16=k16, v16=v16, qi=qi, ki32=ki32, ki16=ki16, wi=wi, ga=ga, u=u, gc=gc)


TQ = 256
KC = 256


def _prompt_attn_kernel(qi_ref, wi_ref, q_ref, ga_ref, kip_ref, kp_ref, vtp_ref, kim_ref, km_ref, vtm_ref,
                        o_ref, sc_ref, scm_ref, wt_ref, s_ref, m_ref, l_ref, acc_ref, *, n_sel):
    qb = pl.program_id(1)
    n_chunks = (qb * TQ + TQ - 1) // KC + 1
    q_pos = qb * TQ + lax.broadcasted_iota(I32, (1, TQ), 1)
    row_kc = lax.broadcasted_iota(I32, (KC, TQ), 0)
    row_m = lax.broadcasted_iota(I32, (LANES, TQ), 0)

    wt_ref[...] = jnp.transpose(wi_ref[...])

    def idx_scores(ki):
        acc = jnp.zeros((ki.shape[0], TQ), F32)
        for h in range(N_IDX_HEADS):
            s = _dot_t(ki, qi_ref[:, h * D_IDX:(h + 1) * D_IDX])
            acc = acc + jnp.maximum(s, 0.0) * wt_ref[h:h + 1, :]
        return acc

    sm = idx_scores(kim_ref[...])
    meta_vis = row_m < N_META
    scm_ref[...] = jnp.where(meta_vis, sm, -jnp.inf)

    def score_chunk(j, carry):
        rmax, rmin = carry
        r0 = pl.multiple_of(j * KC, KC)
        s = idx_scores(kip_ref[pl.ds(r0, KC), :])
        vis = (j * KC + row_kc) <= q_pos
        s_vis = jnp.where(vis, s, -jnp.inf)
        sc_ref[j] = s_vis
        rmax = jnp.maximum(rmax, _fold_rows(s_vis, jnp.max))
        rmin = jnp.minimum(rmin, _fold_rows(jnp.where(vis, s, jnp.inf), jnp.min))
        return rmax, rmin

    rmax, rmin = lax.fori_loop(0, n_chunks, score_chunk,
                               (_fold_rows(jnp.where(meta_vis, sm, -jnp.inf), jnp.max),
                                _fold_rows(jnp.where(meta_vis, sm, jnp.inf), jnp.min)))

    def count(f):
        def ones(mask):
            return _fold_rows(jnp.where(mask, 1, 0), jnp.sum)

        def body(j, c):
            return c + ones(f(sc_ref[j], LANES + j * KC + row_kc))

        c = lax.fori_loop(0, n_chunks, body, ones(f(scm_ref[...], row_m)))
        return jnp.sum(c, axis=0, keepdims=True)

    n_vis = N_META + 1 + q_pos
    thr, j_max = _topk_threshold(count, jnp.min(rmin, axis=0, keepdims=True),
                                 _above(jnp.max(rmax, axis=0, keepdims=True)), n_vis,
                                 n_vis > n_sel, n_sel, LANES + sc_ref.shape[0] * KC)

    def bias_of(s, idx):
        keep = (s > thr) | ((s == thr) & (idx <= j_max))
        return jnp.where(keep, 0.0, NEG_BIG)

    m_ref[...] = jnp.full(m_ref.shape, NEG_BIG, F32)
    l_ref[...] = jnp.zeros(l_ref.shape, F32)
    acc_ref[...] = jnp.zeros(acc_ref.shape, F32)

    def attend(bias, k_blk, vt_blk):
        n = bias.shape[0]
        heads = [slice(h * HEAD_DIM, (h + 1) * HEAD_DIM) for h in range(N_HEADS)]
        m_new = []
        for h, cs in enumerate(heads):
            s = _dot_t(k_blk[:, cs], q_ref[:, cs]) + bias
            s_ref[h, 0:n, :] = s
            m_new.append(jnp.maximum(m_ref[h], _col_reduce(s, jnp.max)))
        for h, cs in enumerate(heads):
            alpha = jnp.exp(m_ref[h] - m_new[h])
            p = jnp.exp(s_ref[h, 0:n, :] - m_new[h])
            l_ref[h] = alpha * l_ref[h] + _col_reduce(p, jnp.sum)
            pv = jnp.dot(vt_blk[cs, :], p.astype(BF16), preferred_element_type=F32)
            acc_ref[cs, :] = alpha * acc_ref[cs, :] + pv
            m_ref[h] = m_new[h]

    attend(bias_of(scm_ref[...], row_m), km_ref[...], vtm_ref[...])

    def attn_chunk(j, carry):
        r0 = pl.multiple_of(j * KC, KC)
        attend(bias_of(sc_ref[j], LANES + j * KC + row_kc), kp_ref[pl.ds(r0, KC), :], vtp_ref[j])
        return carry

    lax.fori_loop(0, n_chunks, attn_chunk, 0)

    for h in range(N_HEADS):
        cs = slice(h * HEAD_DIM, (h + 1) * HEAD_DIM)
        out = jnp.transpose(acc_ref[cs, :] / l_ref[h])
        o_ref[:, cs] = (out * ga_ref[:, cs].astype(F32)).astype(o_ref.dtype)


def _prompt_attention(p, meta, n_batch, seq, n_sel):
    def rows(a):
        return a.reshape(n_batch, seq, a.shape[-1])

    qi, wi, q, ga, ki16, k16 = (rows(p[n]) for n in ("qi", "wi", "q", "ga", "ki16", "k16"))
    n_kc = seq // KC
    vt = jnp.transpose(p["v16"].reshape(n_batch, n_kc, KC, D_ATTN), (0, 1, 3, 2))
    blk = lambda w: pl.BlockSpec((None, TQ, w), lambda b, i: (b, i, 0))
    res = lambda w: pl.BlockSpec((None, seq, w), lambda b, i: (b, 0, 0), pipeline_mode=pl.Buffered(1))
    const = lambda a: pl.BlockSpec(a.shape, lambda b, i: (0, 0), pipeline_mode=pl.Buffered(1))
    return pl.pallas_call(
        functools.partial(_prompt_attn_kernel, n_sel=n_sel),
        out_shape=jax.ShapeDtypeStruct((n_batch, seq, D_ATTN), BF16),
        grid=(n_batch, seq // TQ),
        in_specs=[blk(N_IDX_HEADS * D_IDX), blk(LANES), blk(D_ATTN), blk(D_ATTN),
                  res(D_IDX), res(D_ATTN),
                  pl.BlockSpec((None, n_kc, D_ATTN, KC), lambda b, i: (b, 0, 0, 0), pipeline_mode=pl.Buffered(1)),
                  const(meta["ki"]), const(meta["k"]), const(meta["vt"])],
        out_specs=blk(D_ATTN),
        scratch_shapes=[pltpu.VMEM((n_kc, KC, TQ), F32),
                        pltpu.VMEM((LANES, TQ), F32),
                        pltpu.VMEM((LANES, TQ), F32),
                        pltpu.VMEM((N_HEADS, KC, TQ), F32),
                        pltpu.VMEM((N_HEADS, 1, TQ), F32),
                        pltpu.VMEM((N_HEADS, 1, TQ), F32),
                        pltpu.VMEM((D_ATTN, TQ), F32)],
        compiler_params=_params("parallel", "arbitrary"),
        name="prompt_attn",
    )(qi, wi, q, ga, ki16, k16, vt, meta["ki"], meta["k"], meta["vt"])


SCORE_PAGES = 32
SCORE_GROUP = 8


def _idx_rows(qi, wcol, ki, n_q):
    wcol = jnp.concatenate([wcol] * (ki.shape[0] // wcol.shape[1]), axis=1)
    a = jnp.maximum(_dot_t(qi, ki), 0.0) * wcol
    return jnp.sum(a.reshape(n_q, N_IDX_HEADS, ki.shape[0]), axis=1)


def _sample_score_kernel(pt_ref, qi_ref, wcol_ref, kin_ref, cache_ref, o_ref, on_ref, buf_ref, sem_ref, *, layer):
    steps = pl.num_programs(1)
    g = pl.program_id(0) * steps + pl.program_id(1)
    slot = lax.rem(g, 2)

    def page_copy(page, r, s):
        return pltpu.make_async_copy(cache_ref.at[layer, page], buf_ref.at[s, r], sem_ref.at[s])

    def fetch(g_next, s):
        b_next, j_next = g_next // steps, lax.rem(g_next, steps)

        def start(r, carry):
            page_copy(pt_ref[b_next, j_next * SCORE_PAGES + r], r, s).start()
            return carry

        lax.fori_loop(0, SCORE_PAGES, start, 0)

    @pl.when(g == 0)
    def _():
        fetch(g, slot)

    @pl.when(g + 1 < pl.num_programs(0) * steps)
    def _():
        fetch(g + 1, 1 - slot)

    def wait(r, carry):
        page_copy(0, r, slot).wait()
        return carry

    lax.fori_loop(0, SCORE_PAGES, wait, 0)

    n_q = o_ref.shape[0]
    qi = qi_ref[...]
    wcol = wcol_ref[...]
    for r0 in range(0, SCORE_PAGES, SCORE_GROUP):
        keys = jnp.concatenate([buf_ref[slot, r0 + r].astype(BF16) for r in range(SCORE_GROUP)], axis=0)
        o_ref[:, r0 * PAGE_SIZE:(r0 + SCORE_GROUP) * PAGE_SIZE] = _idx_rows(qi, wcol, keys, n_q)

    @pl.when(pl.program_id(1) == steps - 1)
    def _():
        sc = _idx_rows(qi, wcol, kin_ref[...], n_q)
        lane = lax.broadcasted_iota(I32, (n_q, LANES), 1)
        row = lax.broadcasted_iota(I32, (n_q, LANES), 0)
        on_ref[...] = jnp.where(lane <= row, sc, -jnp.inf)


def _sample_scores(page_table, qi_rows, wcol, ki_new_pad, cache_kidx, layer):
    db, n_pages = page_table.shape
    n_rows = qi_rows.shape[1]
    n_q = n_rows // N_IDX_HEADS
    steps = n_pages // SCORE_PAGES
    per_b = lambda *shape: pl.BlockSpec((None,) + shape, lambda b, j, pt: (b, 0, 0))
    grid_spec = pltpu.PrefetchScalarGridSpec(
        num_scalar_prefetch=1,
        grid=(db, steps),
        in_specs=[per_b(n_rows, D_IDX), per_b(n_rows, LANES), per_b(LANES, D_IDX),
                  pl.BlockSpec(memory_space=pl.ANY)],
        out_specs=[pl.BlockSpec((None, n_q, SCORE_PAGES * PAGE_SIZE), lambda b, j, pt: (b, 0, j)),
                   per_b(n_q, LANES)],
        scratch_shapes=[pltpu.VMEM((2, SCORE_PAGES, PAGE_SIZE, D_IDX), F32), pltpu.SemaphoreType.DMA((2,))],
    )
    return pl.pallas_call(
        functools.partial(_sample_score_kernel, layer=layer),
        out_shape=[jax.ShapeDtypeStruct((db, n_q, n_pages * PAGE_SIZE), F32),
                   jax.ShapeDtypeStruct((db, n_q, LANES), F32)],
        grid_spec=grid_spec,
        compiler_params=_params("arbitrary", "arbitrary"),
        name="sample_scores",
    )(page_table, qi_rows, wcol, ki_new_pad, cache_kidx)


def _sample_thr_kernel(sp_ref, sn_ref, thr_ref, j_ref, *, n_sel, n_q):
    rows, past = sp_ref.shape
    lane = lax.broadcasted_iota(I32, (rows, LANES), 1)
    idx_p = lax.broadcasted_iota(I32, (rows, past), 1)
    sn = sn_ref[...]

    def count(f):
        c = jnp.sum(jnp.where(f(sp_ref[...], idx_p), 1, 0), axis=1, keepdims=True)
        return c + jnp.sum(jnp.where(f(sn, past + lane), 1, 0), axis=1, keepdims=True)

    new_vis = sn > -jnp.inf
    rmax = jnp.maximum(jnp.max(sp_ref[...], axis=1, keepdims=True), jnp.max(sn, axis=1, keepdims=True))
    rmin = jnp.minimum(jnp.min(sp_ref[...], axis=1, keepdims=True),
                       jnp.min(jnp.where(new_vis, sn, jnp.inf), axis=1, keepdims=True))
    n_vis = past + 1 + lax.rem(lax.broadcasted_iota(I32, (rows, 1), 0), n_q)
    thr, j_max = _topk_threshold(count, rmin, _above(rmax), n_vis, n_vis > n_sel, n_sel, past + LANES)
    thr_ref[...] = jnp.broadcast_to(thr, thr_ref.shape)
    j_ref[...] = jnp.broadcast_to(j_max, j_ref.shape)


def _sample_threshold(scores_past, scores_new, n_sel):
    db, n_q, past = scores_past.shape
    rows = db * n_q
    out = jax.ShapeDtypeStruct((rows, LANES), F32), jax.ShapeDtypeStruct((rows, LANES), I32)
    thr, j_max = pl.pallas_call(
        functools.partial(_sample_thr_kernel, n_sel=n_sel, n_q=n_q),
        out_shape=out,
        compiler_params=pltpu.CompilerParams(vmem_limit_bytes=VMEM_LIMIT),
        name="sample_threshold",
    )(scores_past.reshape(rows, past), scores_new.reshape(rows, LANES))
    return thr.reshape(db, n_q, LANES), j_max.reshape(db, n_q, LANES)


SC_LANES = 16
SC_WORKERS = 32
GATHER_ROWS = 32


def _sc_gather_kernel(sc_hbm, thr_hbm, jmax_hbm, pt_hbm, kc_hbm, vc_hbm, ksel_hbm, vsel_hbm, cnt_hbm,
                      sc_v, thr_v, jmax_v, pt_v, list_v, cnt_v, buf0, buf1, sem0, sem1,
                      *, n_q, past, n_pages, n_sel, rows_per_worker, n_cores):
    wid = lax.axis_index("s") * n_cores + lax.axis_index("c")
    lanes = lax.iota(I32, SC_LANES)
    zero = jnp.zeros((SC_LANES,), I32)
    bufs, sems = (buf0, buf1), (sem0, sem1)

    @pl.loop(0, rows_per_worker)
    def _(t):
        r = wid * rows_per_worker + t
        pltpu.sync_copy(pt_hbm.at[pl.ds((r // n_q) * n_pages, n_pages)], pt_v)
        pltpu.sync_copy(sc_hbm.at[pl.ds(r * past, past)], sc_v)
        pltpu.sync_copy(thr_hbm.at[pl.ds(r * LANES, SC_LANES)], thr_v)
        pltpu.sync_copy(jmax_hbm.at[pl.ds(r * LANES, SC_LANES)], jmax_v)
        thr = thr_v[...]
        j_max = jmax_v[...]
        for i in range(n_sel // SC_LANES):
            list_v[pl.ds(i * SC_LANES, SC_LANES)] = zero

        def body(i, cnt):
            base = i * SC_LANES
            s = sc_v[pl.ds(base, SC_LANES)]
            idx = base + lanes
            keep = (s > thr) | ((s == thr) & (idx <= j_max))
            page = plsc.load_gather(pt_v, [lax.shift_right_logical(idx, 7)])
            phys = page * PAGE_SIZE + (idx & (PAGE_SIZE - 1))
            pos = cnt + plsc.cumsum(jnp.where(keep, 1, 0)) - 1
            keep = keep & (pos < n_sel)
            plsc.store_scatter(list_v, [pos], phys, mask=keep)
            return cnt + plsc.all_reduce_population_count(keep)

        cnt = lax.fori_loop(0, past // SC_LANES, body, zero)
        cnt_v[...] = cnt
        pltpu.sync_copy(cnt_v, cnt_hbm.at[pl.ds(r * SC_LANES, SC_LANES)])

        n_pieces = n_sel // GATHER_ROWS
        jobs = [(kc_hbm, ksel_hbm, p) for p in range(n_pieces)] + [(vc_hbm, vsel_hbm, p) for p in range(n_pieces)]

        def gather(job, slot):
            src, _, p = job
            return pltpu.async_copy(src.at[list_v.at[pl.ds(p * GATHER_ROWS, GATHER_ROWS)]], bufs[slot], sems[slot])

        pending = gather(jobs[0], 0)
        for n, job in enumerate(jobs):
            pending.wait()
            if n + 1 < len(jobs):
                pending = gather(jobs[n + 1], (n + 1) % 2)
            _, dst, p = job
            pltpu.sync_copy(bufs[n % 2], dst.at[pl.ds(r * n_sel + p * GATHER_ROWS, GATHER_ROWS)])


def _sample_gather(scores_past, thr, j_max, page_table, cache_k_rows, cache_v_rows, n_sel):
    db, n_q, past = scores_past.shape
    rows = db * n_q
    n_pages = page_table.shape[1]
    info = plsc.get_sparse_core_info()
    assert info.num_lanes == SC_LANES and info.num_cores * info.num_subcores == SC_WORKERS
    assert rows % SC_WORKERS == 0 and n_sel % GATHER_ROWS == 0 and past % SC_LANES == 0
    mesh = plsc.VectorSubcoreMesh(core_axis_name="c", subcore_axis_name="s")
    sel = jax.ShapeDtypeStruct((rows * n_sel, N_HEADS, HEAD_DIM), F32)
    body = functools.partial(_sc_gather_kernel, n_q=n_q, past=past, n_pages=n_pages, n_sel=n_sel,
                             rows_per_worker=rows // SC_WORKERS, n_cores=info.num_cores)
    return pl.kernel(
        body,
        out_type=[sel, sel, jax.ShapeDtypeStruct((rows * SC_LANES,), I32)],
        mesh=mesh,
        scratch_types=[pltpu.VMEM((past,), F32), pltpu.VMEM((SC_LANES,), F32), pltpu.VMEM((SC_LANES,), I32),
                       pltpu.VMEM((n_pages,), I32), pltpu.VMEM((n_sel,), I32), pltpu.VMEM((SC_LANES,), I32),
                       pltpu.VMEM((GATHER_ROWS, N_HEADS, HEAD_DIM), F32),
                       pltpu.VMEM((GATHER_ROWS, N_HEADS, HEAD_DIM), F32),
                       pltpu.SemaphoreType.DMA, pltpu.SemaphoreType.DMA],
        compiler_params=pltpu.CompilerParams(needs_layout_passes=False),
        name="sample_gather",
    )(scores_past.reshape(-1), thr.reshape(-1), j_max.reshape(-1), page_table.reshape(-1),
      cache_k_rows, cache_v_rows)


def _sample_attn_kernel(cnt_ref, q_ref, k_ref, v_ref, sn_ref, thr_ref, jmax_ref, kn_ref, vn_ref, ga_ref, o_ref,
                        *, past, n_sel):
    r = pl.program_id(0)
    q8 = q_ref[...]

    def flat(rows_ref):
        heads = [rows_ref[pl.ds(h, n_sel, stride=N_HEADS), :] for h in range(N_HEADS)]
        return jnp.concatenate(heads, axis=1).astype(BF16)

    slot = lax.broadcasted_iota(I32, (N_HEADS, n_sel), 1)
    s_p = jnp.where(slot < cnt_ref[r], _dot_t(q8, flat(k_ref)), NEG_BIG)
    sn = sn_ref[...]
    idx_n = past + lax.broadcasted_iota(I32, sn.shape, 1)
    thr = thr_ref[:, 0:1]
    keep_n = (sn > thr) | ((sn == thr) & (idx_n <= jmax_ref[:, 0:1]))
    s_n = _dot_t(q8, kn_ref[...]) + jnp.where(keep_n, 0.0, NEG_BIG)

    m = jnp.maximum(jnp.max(s_p, axis=1, keepdims=True), jnp.max(s_n, axis=1, keepdims=True))
    p_p = jnp.exp(s_p - m)
    p_n = jnp.exp(s_n - m)
    den = jnp.sum(p_p, axis=1, keepdims=True) + jnp.sum(p_n, axis=1, keepdims=True)
    out8 = jnp.dot(p_p.astype(BF16), flat(v_ref), preferred_element_type=F32)
    out8 = (out8 + jnp.dot(p_n.astype(BF16), vn_ref[...], preferred_element_type=F32)) / den
    row = lax.broadcasted_iota(I32, out8.shape, 0)
    col_head = lax.broadcasted_iota(I32, out8.shape, 1) // HEAD_DIM
    o_ref[...] = jnp.sum(jnp.where(row == col_head, out8, 0.0), axis=0, keepdims=True) * ga_ref[...]


def _sample_attention(cnt, q8, k_sel, v_sel, scores_new, thr, j_max, k_new_pad, v_new_pad, ga, n_q, past, n_sel):
    rows = q8.shape[0]
    row3 = lambda *shape: pl.BlockSpec((None,) + shape, lambda r, c: (r, 0, 0))
    per_b = lambda *shape: pl.BlockSpec((None,) + shape, lambda r, c: (r // n_q, 0, 0))
    grid_spec = pltpu.PrefetchScalarGridSpec(
        num_scalar_prefetch=1,
        grid=(rows,),
        in_specs=[row3(N_HEADS, D_ATTN), row3(n_sel * N_HEADS, HEAD_DIM), row3(n_sel * N_HEADS, HEAD_DIM),
                  row3(1, LANES), row3(1, LANES), row3(1, LANES),
                  per_b(LANES, D_ATTN), per_b(LANES, D_ATTN), row3(1, D_ATTN)],
        out_specs=row3(1, D_ATTN),
    )
    as_rows = lambda a: a.reshape(rows, 1, a.shape[-1])
    return pl.pallas_call(
        functools.partial(_sample_attn_kernel, past=past, n_sel=n_sel),
        out_shape=jax.ShapeDtypeStruct((rows, 1, D_ATTN), F32),
        grid_spec=grid_spec,
        compiler_params=_params("parallel"),
        name="sample_attn",
    )(cnt, q8, k_sel.reshape(rows, n_sel * N_HEADS, HEAD_DIM), v_sel.reshape(rows, n_sel * N_HEADS, HEAD_DIM),
      as_rows(scores_new), as_rows(thr), as_rows(j_max), k_new_pad, v_new_pad, as_rows(ga))


CONV_TC = 256
CONV_HALO = 32
CONV_RB = 32
CONV_CB = 512
CONV_SHIFT_ROWS = 40


def _conv_finish(y, cb, g, b, gate):
    y = _layer_norm(y + cb, g, b)
    return y * _sigmoid(y) * gate


def _conv_prompt_kernel(u_ref, prev_ref, head_ref, gc_ref, w_ref, cb_ref, g_ref, b_ref, o_ref, win_ref, sh_ref, y_ref):
    i = pl.program_id(1)

    @pl.when(i == 0)
    def _():
        win_ref[0:CONV_HALO, :] = head_ref[...]

    @pl.when(i > 0)
    def _():
        win_ref[0:CONV_HALO, :] = prev_ref[...]

    win_ref[CONV_HALO:, :] = u_ref[...]
    n_sh = sh_ref.shape[1]
    for s in range(1, SUBLANES):
        for r in range(0, n_sh, CONV_SHIFT_ROWS):
            sh_ref[s - 1, r:r + CONV_SHIFT_ROWS, :] = win_ref[r + s:r + s + CONV_SHIFT_ROWS, :]
    off = CONV_HALO - (CONV_W - 1)
    n_ch = u_ref.shape[1]
    for r0 in range(0, CONV_TC, CONV_RB):
        for c0 in range(0, n_ch, CONV_CB):
            cs = slice(c0, c0 + CONV_CB)
            acc = jnp.zeros((CONV_RB, CONV_CB), F32)
            for t in range(CONV_W):
                s, base = (off + t) % SUBLANES, r0 + (off + t) // SUBLANES * SUBLANES
                rows = win_ref[base:base + CONV_RB, cs] if s == 0 else sh_ref[s - 1, base:base + CONV_RB, cs]
                acc = acc + rows * w_ref[t:t + 1, cs]
            y_ref[r0:r0 + CONV_RB, cs] = acc
    for r0 in range(0, CONV_TC, CONV_RB):
        rows = slice(r0, r0 + CONV_RB)
        gate = gc_ref[rows, :].astype(F32)
        o_ref[rows, :] = _conv_finish(y_ref[rows, :], cb_ref[...], g_ref[...], b_ref[...], gate).astype(o_ref.dtype)


def _conv_prompt(u, head, gc, conv_w, conv_b, g, b, n_batch, seq):
    c = u.shape[-1]
    u3 = u.reshape(n_batch, seq, c)
    gc3 = gc.reshape(n_batch, seq, c)
    ratio = CONV_TC // CONV_HALO
    vec = lambda: pl.BlockSpec((1, c), lambda bb, i: (0, 0))
    out = pl.pallas_call(
        _conv_prompt_kernel,
        out_shape=jax.ShapeDtypeStruct((n_batch, seq, c), BF16),
        grid=(n_batch, seq // CONV_TC),
        in_specs=[pl.BlockSpec((None, CONV_TC, c), lambda bb, i: (bb, i, 0)),
                  pl.BlockSpec((None, CONV_HALO, c), lambda bb, i: (bb, jnp.maximum(i * ratio - 1, 0), 0)),
                  pl.BlockSpec((CONV_HALO, c), lambda bb, i: (0, 0)),
                  pl.BlockSpec((None, CONV_TC, c), lambda bb, i: (bb, i, 0)),
                  pl.BlockSpec((CONV_W, c), lambda bb, i: (0, 0)),
                  vec(), vec(), vec()],
        out_specs=pl.BlockSpec((None, CONV_TC, c), lambda bb, i: (bb, i, 0)),
        scratch_shapes=[pltpu.VMEM((CONV_HALO + CONV_TC, c), F32),
                        pltpu.VMEM((SUBLANES - 1, CONV_HALO + CONV_TC - SUBLANES, c), F32),
                        pltpu.VMEM((CONV_TC, c), F32)],
        compiler_params=_params("parallel", "arbitrary"),
        name="conv_prompt",
    )(u3, u3, head, gc3, conv_w, conv_b.reshape(1, c), g.reshape(1, c), b.reshape(1, c))
    return out.reshape(n_batch * seq, c)


def _conv_sample_kernel(st_ref, u_ref, gc_ref, w_ref, cb_ref, g_ref, b_ref, o_ref):
    n_hist = st_ref.shape[0]
    n_new = u_ref.shape[0]
    for r in range(n_new):
        acc = jnp.zeros(u_ref.shape[1:], F32)
        for t in range(CONV_W):
            src = r + t
            row = st_ref[src] if src < n_hist else u_ref[src - n_hist]
            acc = acc + row * w_ref[t:t + 1, :]
        o_ref[r] = _conv_finish(acc, cb_ref[...], g_ref[...], b_ref[...], gc_ref[r])


def _conv_sample(state_t, u_t, gc_t, conv_w, conv_b, g, b):
    c = u_t.shape[-1]
    return pl.pallas_call(
        _conv_sample_kernel,
        out_shape=jax.ShapeDtypeStruct(u_t.shape, F32),
        compiler_params=pltpu.CompilerParams(vmem_limit_bytes=VMEM_LIMIT),
        name="conv_sample",
    )(state_t, u_t, gc_t, conv_w, conv_b.reshape(1, c), g.reshape(1, c), b.reshape(1, c))


def _out_kernel(x_ref, a_ref, c_ref, wa32_ref, wc32_ref, gi_ref, bi_ref, go_ref, bo_ref, o_ref, wa_ref, wc_ref,
                *, alpha):
    _cast_once([wa32_ref, wc32_ref], [wa_ref, wc_ref])
    h = _layer_norm(x_ref[...], gi_ref[...], bi_ref[...])
    z = jnp.dot(a_ref[...], wa_ref[...], preferred_element_type=F32)
    z = z + jnp.dot(c_ref[...], wc_ref[...], preferred_element_type=F32)
    o_ref[...] = _layer_norm(alpha * h + z, go_ref[...], bo_ref[...])


def _out_projection(x, a, c, w_out, gi, bi, go, bo, alpha, tm):
    m, d = x.shape
    n_a, n_c = a.shape[1], c.shape[1]
    assert n_a == n_c and w_out.shape == (n_a + n_c, d)
    row = lambda w: pl.BlockSpec((tm, w), lambda i: (i, 0))
    full = lambda arr: pl.BlockSpec(arr.shape, lambda i: (0, 0))
    w_blk = lambda j: pl.BlockSpec((n_a, d), lambda i: (j, 0), pipeline_mode=pl.Buffered(1))
    vecs = [v.reshape(1, d) for v in (gi, bi, go, bo)]
    return pl.pallas_call(
        functools.partial(_out_kernel, alpha=alpha),
        out_shape=jax.ShapeDtypeStruct((m, d), F32),
        grid=(m // tm,),
        in_specs=[row(d), row(n_a), row(n_c), w_blk(0), w_blk(1)] + [full(v) for v in vecs],
        out_specs=row(d),
        scratch_shapes=[pltpu.VMEM((n_a, d), BF16), pltpu.VMEM((n_c, d), BF16)],
        compiler_params=_params("arbitrary"),
        name="out_proj",
    )(x, a, c, w_out, w_out, *vecs)


def _split_w_in(w_t):
    assert N_IDX_HEADS * D_IDX == 2 * D_ATTN
    d_conv = (w_t.shape[0] - (3 * D_ATTN + N_IDX_HEADS * D_IDX + D_IDX + N_IDX_HEADS + D_ATTN)) // 3
    start = 3 * D_ATTN + N_IDX_HEADS * D_IDX + D_IDX + N_IDX_HEADS
    return dict(rest=w_t[start:].astype(BF16), d_conv=d_conv)


def kernel(x_prompt, x_sample, cache_k, cache_v, cache_kidx, state_conv, page_table, meta_tokens,
           ln_in_g, ln_in_b, w_in, ln_kidx_g, ln_kidx_b, conv_w, conv_b, ln_conv_g, ln_conv_b,
           w_out, ln_out_g, ln_out_b):
    n_batch, seq, d_model = x_prompt.shape
    db, n_new, _ = x_sample.shape
    depth = w_in.shape[0]
    assert depth == 1, "one mixer layer per step"
    assert seq >= CONV_W - 1
    n_pages = page_table.shape[1]
    past = n_pages * PAGE_SIZE
    n_sel_prompt = min(TOPK_MAX, seq // 4)
    n_sel_sample = min(TOPK_MAX, (past + n_new) // 4)
    alpha = (2.0 * depth) ** 0.25
    l = 0

    w_t = jnp.transpose(w_in[l])
    w = _split_w_in(w_t)
    d_conv = w["d_conv"]

    xp = x_prompt.reshape(n_batch * seq, d_model)
    xs = jnp.concatenate([meta_tokens.astype(F32), x_sample.reshape(db * n_new, d_model)], axis=0)
    n_small = xs.shape[0]
    pp = _in_projection(xp, ln_in_g, ln_in_b, w_t, w, ln_kidx_g[l], ln_kidx_b[l], 512)
    ps = _in_projection(xs, ln_in_g, ln_in_b, w_t, w, ln_kidx_g[l], ln_kidx_b[l], n_small)
    pm = {n: a[:N_META] for n, a in ps.items()}
    ps = {n: a[N_META:] for n, a in ps.items()}

    pad_rows = lambda a: jnp.pad(a, ((0, LANES - a.shape[0]), (0, 0)))
    meta = dict(ki=pad_rows(pm["ki16"]), k=pad_rows(pm["k16"]), vt=jnp.transpose(pad_rows(pm["v16"])))
    a_p = _prompt_attention(pp, meta, n_batch, seq, n_sel_prompt).reshape(n_batch * seq, D_ATTN)
    head = jnp.concatenate([jnp.zeros((CONV_HALO - N_META, d_conv), F32), pm["u"]], axis=0)
    c_p = _conv_prompt(pp["u"], head, pp["gc"], conv_w[l], conv_b[l], ln_conv_g[l], ln_conv_b[l], n_batch, seq)
    y_p = _out_projection(xp, a_p, c_p, w_out[l], ln_in_g, ln_in_b, ln_out_g[l], ln_out_b[l], alpha, 512)
    y_prompt = y_p.reshape(n_batch, seq, d_model)

    def with_meta(m_rows, p_rows):
        m_b = jnp.broadcast_to(m_rows[None], (n_batch,) + m_rows.shape)
        return jnp.concatenate([m_b, p_rows.reshape(n_batch, seq, -1)], axis=1)

    new_k_p = with_meta(pm["k32"], pp["k32"]).reshape(1, n_batch, N_META + seq, N_HEADS, HEAD_DIM)
    new_v_p = with_meta(pm["v32"], pp["v32"]).reshape(1, n_batch, N_META + seq, N_HEADS, HEAD_DIM)
    new_ki_p = with_meta(pm["ki32"], pp["ki32"])[None]
    new_conv_p = pp["u"].reshape(n_batch, seq, d_conv)[:, -(CONV_W - 1):][None]

    qi_rows = ps["qi"].reshape(db, n_new * N_IDX_HEADS, D_IDX)
    wcol = jnp.broadcast_to(ps["wi"][:, :N_IDX_HEADS].reshape(db, n_new * N_IDX_HEADS, 1),
                            (db, n_new * N_IDX_HEADS, LANES))
    pad_new = lambda a: jnp.pad(a.reshape(db, n_new, a.shape[-1]), ((0, 0), (0, LANES - n_new), (0, 0)))
    sc_past, sc_new = _sample_scores(page_table, qi_rows, wcol, pad_new(ps["ki16"]), cache_kidx, l)
    thr, j_max = _sample_threshold(sc_past, sc_new, n_sel_sample)
    key_rows = lambda c: c.reshape(-1, N_HEADS, HEAD_DIM)
    n_pool = cache_k.shape[1]
    k_sel, v_sel, cnt = _sample_gather(sc_past, thr, j_max, page_table + l * n_pool,
                                       key_rows(cache_k), key_rows(cache_v), n_sel_sample)
    q4 = ps["q"].reshape(db * n_new, N_HEADS, HEAD_DIM)
    q8 = jnp.einsum("rhd,hg->rhgd", q4, jnp.eye(N_HEADS, dtype=BF16)).reshape(db * n_new, N_HEADS, D_ATTN)
    a_s = _sample_attention(cnt.reshape(db * n_new, SC_LANES)[:, 0], q8, k_sel, v_sel, sc_new, thr, j_max,
                            pad_new(ps["k16"]), pad_new(ps["v16"]),
                            ps["ga"].astype(F32), n_new, past, n_sel_sample)
    a_s = a_s.reshape(db * n_new, D_ATTN).astype(BF16)

    to_t = lambda a: jnp.transpose(a.reshape(db, -1, d_conv), (1, 0, 2))
    c_t = _conv_sample(to_t(state_conv[l].astype(F32)), to_t(ps["u"]), to_t(ps["gc"].astype(F32)),
                       conv_w[l], conv_b[l], ln_conv_g[l], ln_conv_b[l])
    c_s = jnp.transpose(c_t, (1, 0, 2)).reshape(db * n_new, d_conv).astype(BF16)
    y_s = _out_projection(x_sample.reshape(db * n_new, d_model), a_s, c_s, w_out[l],
                          ln_in_g, ln_in_b, ln_out_g[l], ln_out_b[l], alpha, db * n_new)
    y_sample = y_s.reshape(db, n_new, d_model)

    new_k_s = ps["k32"].reshape(1, db, n_new, N_HEADS, HEAD_DIM)
    new_v_s = ps["v32"].reshape(1, db, n_new, N_HEADS, HEAD_DIM)
    new_ki_s = ps["ki32"].reshape(1, db, n_new, D_IDX)
    u_ext_s = jnp.concatenate([state_conv[l].astype(F32), ps["u"].reshape(db, n_new, d_conv)], axis=1)
    new_conv_s = u_ext_s[:, -(CONV_W - 1):][None]

    return (y_prompt, y_sample, new_k_p, new_v_p, new_ki_p, new_conv_p,
            new_k_s, new_v_s, new_ki_s, new_conv_s)
```
